```python
import math
import jax, jax.numpy as jnp
from jax import lax
import numpy as np

D_MODEL = 2048
BATCH = 2
SEQ = 8192
DEPTH = 2

A_HEADS = 8
A_KV_HEADS = 2
A_HEAD_DIM = 64
A_WIDTH = A_HEADS * A_HEAD_DIM
A_KV_WIDTH = A_KV_HEADS * A_HEAD_DIM
WINDOW = 128
A_BLOCK = 128
ROPE_THETA = 10000.0
R_WIDTH = 1024
R_BLOCKS = 8
R_BLOCK_DIM = R_WIDTH // R_BLOCKS
R_C = 8.0
CONV_WIDTH = 4
G_HEADS = 4
G_HEAD_DIM = 128
G_WIDTH = G_HEADS * G_HEAD_DIM
G_CHUNK = 64
MIX_WIDTH = A_WIDTH + R_WIDTH + G_WIDTH
IN_SIZES = (A_WIDTH, A_KV_WIDTH, A_KV_WIDTH, A_WIDTH, R_WIDTH, R_WIDTH,
            G_WIDTH, G_WIDTH, G_WIDTH, G_WIDTH, G_HEADS, G_HEADS)
N_IN = sum(IN_SIZES)
DEEPNORM_ALPHA = (2 * DEPTH) ** 0.25
DEEPNORM_BETA = (8 * DEPTH) ** -0.25
LN_EPS = 1e-5
RMS_EPS = 1e-6

kernel_name = "hybrid_swa_rglru_gdn_deepnorm"


def layer_norm(x, g, b):
    xf = x.astype(jnp.float32)
    mu = jnp.mean(xf, -1, keepdims=True)
    var = jnp.mean(jnp.square(xf - mu), -1, keepdims=True)
    return ((xf - mu) * lax.rsqrt(var + LN_EPS) * g.astype(jnp.float32) + b.astype(jnp.float32)).astype(x.dtype)


def rope_tables(seq, dim):
    inv = 1.0 / (ROPE_THETA ** (jnp.arange(0, dim, 2, dtype=jnp.float32) / dim))
    ang = jnp.arange(seq, dtype=jnp.float32)[:, None] * inv[None, :]
    return jnp.cos(ang), jnp.sin(ang)


def apply_rope(x, cos, sin):
    xf = x.astype(jnp.float32)
    x1, x2 = jnp.split(xf, 2, axis=-1)
    c = cos[None, :, None, :]
    s = sin[None, :, None, :]
    return jnp.concatenate([x1 * c - x2 * s, x2 * c + x1 * s], axis=-1).astype(x.dtype)


def causal_depthwise_conv(x, w):
    return lax.conv_general_dilated(
        x, w[:, None, :].astype(x.dtype), window_strides=(1,),
        padding=[(CONV_WIDTH - 1, 0)], dimension_numbers=("NWC", "WIO", "NWC"),
        feature_group_count=x.shape[-1])


def sliding_window_attention(q, k, v, sinks):
    B, S, _, D = q.shape
    nb = S // A_BLOCK
    grp = A_HEADS // A_KV_HEADS
    qb = q.reshape(B, nb, A_BLOCK, A_KV_HEADS, grp, D)
    pad = ((0, 0), (A_BLOCK, 0), (0, 0), (0, 0))
    kp = jnp.pad(k, pad).reshape(B, nb + 1, A_BLOCK, A_KV_HEADS, D)
    vp = jnp.pad(v, pad).reshape(B, nb + 1, A_BLOCK, A_KV_HEADS, D)
    kw = jnp.concatenate([kp[:, :-1], kp[:, 1:]], axis=2)
    vw = jnp.concatenate([vp[:, :-1], vp[:, 1:]], axis=2)
    scores = jnp.einsum("bnqhgd,bnkhd->bhgnqk", qb, kw).astype(jnp.float32) * (D ** -0.5)
    i = jnp.arange(A_BLOCK)[:, None]
    j = jnp.arange(2 * A_BLOCK)[None, :]
    diff = i - j + A_BLOCK
    band = (diff >= 0) & (diff < WINDOW)
    kpos = (jnp.arange(nb)[:, None, None] - 1) * A_BLOCK + j[None]
    mask = band[None] & (kpos >= 0)
    scores = jnp.where(mask, scores, -jnp.inf)
    sink = jnp.broadcast_to(
        sinks.astype(jnp.float32).reshape(1, A_KV_HEADS, grp, 1, 1, 1),
        scores.shape[:-1] + (1,))
    probs = jax.nn.softmax(jnp.concatenate([scores, sink], axis=-1), axis=-1)[..., :-1]
    out = jnp.einsum("bhgnqk,bnkhd->bnqhgd", probs.astype(v.dtype), vw)
    return out.reshape(B, S, A_HEADS * D)


def rg_lru(x, w_a, b_a, w_x, b_x, lam):
    B, S, _ = x.shape
    xb = x.reshape(B, S, R_BLOCKS, R_BLOCK_DIM)
    r = jax.nn.sigmoid(jnp.einsum("bsnc,ncd->bsnd", xb, w_a).reshape(B, S, R_WIDTH) + b_a)
    ig = jax.nn.sigmoid(jnp.einsum("bsnc,ncd->bsnd", xb, w_x).reshape(B, S, R_WIDTH) + b_x)
    log_a = -R_C * r.astype(jnp.float32) * jax.nn.softplus(-lam.astype(jnp.float32))
    a = jnp.exp(log_a)
    u = jnp.sqrt(-jnp.expm1(2.0 * log_a)) * (ig * x).astype(jnp.float32)

    def combine(left, right):
        a1, b1 = left
        a2, b2 = right
        return a1 * a2, a2 * b1 + b2

    _, h = lax.associative_scan(combine, (a, u), axis=1)
    return h.astype(x.dtype)


def gated_delta_chunked(q, k, v, g, beta):
    B, S, H, Dk = q.shape
    Dv = v.shape[-1]
    N = S // G_CHUNK
    C = G_CHUNK

    def chunks(t):
        return t.reshape(B, N, C, H, -1).transpose(0, 3, 1, 2, 4)

    q, k, v = chunks(q), chunks(k), chunks(v)
    g = jnp.cumsum(g.reshape(B, N, C, H).transpose(0, 3, 1, 2), axis=-1)
    beta = beta.reshape(B, N, C, H).transpose(0, 3, 1, 2)
    tril = jnp.tril(jnp.ones((C, C), dtype=bool))
    strict = jnp.tril(jnp.ones((C, C), dtype=bool), -1)
    decay = jnp.exp(jnp.where(tril, g[..., :, None] - g[..., None, :], -jnp.inf))
    kb = k * beta[..., None]
    vb = v * beta[..., None]
    m = jnp.where(strict, jnp.einsum("bhncd,bhnjd->bhncj", kb, k) * decay, 0.0)
    lhs = jnp.eye(C, dtype=jnp.float32) + m
    u = lax.linalg.triangular_solve(lhs, vb, left_side=True, lower=True, unit_diagonal=True)
    w = lax.linalg.triangular_solve(lhs, kb * jnp.exp(g)[..., None], left_side=True,
                                    lower=True, unit_diagonal=True)
    qk = jnp.where(tril, jnp.einsum("bhncd,bhnjd->bhncj", q, k) * decay, 0.0)
    q_dec = q * jnp.exp(g)[..., None]
    k_dec = k * jnp.exp(g[..., -1:] - g)[..., None]
    g_last = jnp.exp(g[..., -1])

    def step(state, inp):
        qk_i, qd_i, kd_i, u_i, w_i, gl_i = inp
        v_new = u_i - jnp.einsum("bhcd,bhde->bhce", w_i, state)
        o = jnp.einsum("bhcd,bhde->bhce", qd_i, state) + jnp.einsum("bhcj,bhje->bhce", qk_i, v_new)
        state = state * gl_i[..., None, None] + jnp.einsum("bhcd,bhce->bhde", kd_i, v_new)
        return state, o

    xs = tuple(jnp.moveaxis(t, 2, 0) for t in (qk, q_dec, k_dec, u, w, g_last))
    state0 = jnp.zeros((B, H, Dk, Dv), jnp.float32)
    _, o = lax.scan(step, state0, xs)
    return o.transpose(1, 0, 3, 2, 4).reshape(B, S, H, Dv)


def l2norm(x):
    return x * lax.rsqrt(jnp.sum(x * x, -1, keepdims=True) + RMS_EPS)


def hybrid_layer(x, cos, sin, w_in, sinks, r_conv_w, r_conv_b, r_wa, r_ba, r_wx, r_bx, r_lam,
                 g_conv_w, g_a_log, g_dt_bias, g_norm_w, w_out, ln_g, ln_b):
    B, S, _ = x.shape
    proj = x @ w_in
    points = [int(p) for p in np.cumsum(IN_SIZES)[:-1]]
    aq, ak, av, az, rx, rz, gq, gk, gv, gz, gb, ga = jnp.split(proj, points, axis=-1)

    q = apply_rope(aq.reshape(B, S, A_HEADS, A_HEAD_DIM), cos, sin)
    k = apply_rope(ak.reshape(B, S, A_KV_HEADS, A_HEAD_DIM), cos, sin)
    v = av.reshape(B, S, A_KV_HEADS, A_HEAD_DIM)
    y_a = sliding_window_attention(q, k, v, sinks) * jax.nn.silu(az)

    xr = causal_depthwise_conv(rx, r_conv_w) + r_conv_b
    y_r = rg_lru(xr, r_wa, r_ba, r_wx, r_bx, r_lam) * jax.nn.silu(rz)

    qkv = jax.nn.silu(causal_depthwise_conv(jnp.concatenate([gq, gk, gv], axis=-1), g_conv_w))
    cq, ck, cv = jnp.split(qkv.astype(jnp.float32), 3, axis=-1)
    cq = l2norm(cq.reshape(B, S, G_HEADS, G_HEAD_DIM)) * (G_HEAD_DIM ** -0.5)
    ck = l2norm(ck.reshape(B, S, G_HEADS, G_HEAD_DIM))
    cv = cv.reshape(B, S, G_HEADS, G_HEAD_DIM)
    beta = jax.nn.sigmoid(gb.astype(jnp.float32))
    g = -jnp.exp(g_a_log.astype(jnp.float32)) * jax.nn.softplus(
        ga.astype(jnp.float32) + g_dt_bias.astype(jnp.float32))
    o = gated_delta_chunked(cq, ck, cv, g, beta)
    o = o * lax.rsqrt(jnp.mean(o * o, -1, keepdims=True) + RMS_EPS) * g_norm_w.astype(jnp.float32)
    y_g = o.reshape(B, S, G_WIDTH).astype(x.dtype) * jax.nn.silu(gz)

    y = jnp.concatenate([y_a, y_r, y_g], axis=-1) @ w_out
    return layer_norm(DEEPNORM_ALPHA * x + y, ln_g, ln_b)


def setup_inputs(seed: int = 0) -> dict:
    key = jax.random.key(seed)
    ks = jax.random.split(key, 20)
    f32 = jnp.float32
    x = jax.random.normal(ks[0], (BATCH, SEQ, D_MODEL), f32)
    w_in = jax.random.normal(ks[1], (DEPTH, D_MODEL, N_IN), f32) * D_MODEL ** -0.5
    sinks = jax.random.normal(ks[2], (DEPTH, A_HEADS), f32)
    r_conv_w = jax.random.normal(ks[3], (DEPTH, CONV_WIDTH, R_WIDTH), f32) * CONV_WIDTH ** -0.5
    r_conv_b = jax.random.normal(ks[4], (DEPTH, R_WIDTH), f32) * 0.01
    r_wa = jax.random.normal(ks[5], (DEPTH, R_BLOCKS, R_BLOCK_DIM, R_BLOCK_DIM), f32) * R_BLOCK_DIM ** -0.5
    r_ba = jax.random.normal(ks[6], (DEPTH, R_WIDTH), f32) * 0.01
    r_wx = jax.random.normal(ks[7], (DEPTH, R_BLOCKS, R_BLOCK_DIM, R_BLOCK_DIM), f32) * R_BLOCK_DIM ** -0.5
    r_bx = jax.random.normal(ks[8], (DEPTH, R_WIDTH), f32) * 0.01
    a_c = jax.random.uniform(ks[9], (DEPTH, R_WIDTH), f32, minval=0.9, maxval=0.999)
    a0 = a_c ** (1.0 / R_C)
    r_lam = jnp.log(a0) - jnp.log1p(-a0)
    g_conv_w = jax.random.normal(ks[10], (DEPTH, CONV_WIDTH, 3 * G_WIDTH), f32) * CONV_WIDTH ** -0.5
    g_a_log = jnp.log(jax.random.uniform(ks[11], (DEPTH, G_HEADS), f32, minval=1.0, maxval=16.0))
    dt = jnp.exp(jax.random.uniform(ks[12], (DEPTH, G_HEADS), f32,
                                    minval=math.log(1e-3), maxval=math.log(1e-1)))
    g_dt_bias = dt + jnp.log(-jnp.expm1(-dt))
    g_norm_w = 1.0 + 0.01 * jax.random.normal(ks[13], (DEPTH, G_HEAD_DIM), f32)
    w_out = jax.random.normal(ks[14], (DEPTH, MIX_WIDTH, D_MODEL), f32) * (MIX_WIDTH ** -0.5) * DEEPNORM_BETA
    ln_g = 1.0 + 0.01 * jax.random.normal(ks[15], (DEPTH, D_MODEL), f32)
    ln_b = 0.01 * jax.random.normal(ks[16], (DEPTH, D_MODEL), f32)
    return {"x": x, "w_in": w_in, "sinks": sinks, "r_conv_w": r_conv_w, "r_conv_b": r_conv_b,
            "r_wa": r_wa, "r_ba": r_ba, "r_wx": r_wx, "r_bx": r_bx, "r_lam": r_lam,
            "g_conv_w": g_conv_w, "g_a_log": g_a_log, "g_dt_bias": g_dt_bias, "g_norm_w": g_norm_w,
            "w_out": w_out, "ln_g": ln_g, "ln_b": ln_b}


def reference(x, w_in, sinks, r_conv_w, r_conv_b, r_wa, r_ba, r_wx, r_bx, r_lam,
              g_conv_w, g_a_log, g_dt_bias, g_norm_w, w_out, ln_g, ln_b):
    cos, sin = rope_tables(x.shape[1], A_HEAD_DIM)
    for l in range(DEPTH):
        x = hybrid_layer(x, cos, sin, w_in[l], sinks[l], r_conv_w[l], r_conv_b[l], r_wa[l], r_ba[l],
                         r_wx[l], r_bx[l], r_lam[l], g_conv_w[l], g_a_log[l], g_dt_bias[l],
                         g_norm_w[l], w_out[l], ln_g[l], ln_b[l])
    return x
```

```python
import functools
import math

import numpy as np
import jax
import jax.numpy as jnp
from jax import lax
from jax.experimental import pallas as pl
from jax.experimental.pallas import tpu as pltpu

D_MODEL = 2048
DEPTH = 2
A_HEADS = 8
A_KV_HEADS = 2
A_HEAD_DIM = 64
A_WIDTH = A_HEADS * A_HEAD_DIM
A_KV_WIDTH = A_KV_HEADS * A_HEAD_DIM
WINDOW = 128
A_BLOCK = 128
ROPE_THETA = 10000.0
R_WIDTH = 1024
R_BLOCKS = 8
R_BLOCK_DIM = R_WIDTH // R_BLOCKS
R_C = 8.0
CONV_WIDTH = 4
G_HEADS = 4
G_HEAD_DIM = 128
G_WIDTH = G_HEADS * G_HEAD_DIM
G_CHUNK = 64
MIX_WIDTH = A_WIDTH + R_WIDTH + G_WIDTH
DEEPNORM_ALPHA = (2 * DEPTH) ** 0.25
LN_EPS = 1e-5
RMS_EPS = 1e-6

LANES = 128
SUBLANES = 8
VMEM_LIMIT = 56 * 1024 * 1024

COL_RX = 0
COL_RZ = 1024
COL_AQ = 2048
COL_AZ = 2560
COL_GQ = 3072
COL_GK = 3584
COL_GV = 4096
COL_GZ = 4608
COL_KV = 5120
COL_SMALL = 5376
N_PROJ = 5632

F32 = jnp.float32
BF16 = jnp.bfloat16
HIGHEST = lax.Precision.HIGHEST


def _silu(x):
    return x * jax.nn.sigmoid(x)


def _softplus(x):
    return jnp.maximum(x, 0.0) + jnp.log1p(jnp.exp(-jnp.abs(x)))


def _inproj_kernel(x_ref, w_ref, o_ref):
    o_ref[...] = jnp.dot(x_ref[...], w_ref[...], preferred_element_type=F32)


def _inproj(xb, w):
    T = xb.shape[0]
    tm, tn = 1024, 512
    return pl.pallas_call(
        _inproj_kernel,
        grid=(T // tm, N_PROJ // tn),
        in_specs=[pl.BlockSpec((tm, D_MODEL), lambda i, j: (i, 0)),
                  pl.BlockSpec((D_MODEL, tn), lambda i, j: (0, j))],
        out_specs=pl.BlockSpec((tm, tn), lambda i, j: (i, j)),
        out_shape=jax.ShapeDtypeStruct((T, N_PROJ), F32),
        compiler_params=pltpu.CompilerParams(
            dimension_semantics=("parallel", "arbitrary"),
            vmem_limit_bytes=VMEM_LIMIT),
        name="inproj",
    )(xb, w)


def _rope(x, cos, sin_signed):
    w = x.shape[1]
    reps = w // LANES
    if reps > 1:
        cos = jnp.concatenate([cos] * reps, axis=1)
        sin_signed = jnp.concatenate([sin_signed] * reps, axis=1)
    lane = lax.broadcasted_iota(jnp.int32, x.shape, 1)
    first_half = (lane % A_HEAD_DIM) < (A_HEAD_DIM // 2)
    swapped = jnp.where(first_half,
                        pltpu.roll(x, w - A_HEAD_DIM // 2, 1),
                        pltpu.roll(x, A_HEAD_DIM // 2, 1))
    return x * cos + swapped * sin_signed


def _attn_kernel(sinks_ref, q_ref, kvc_ref, kvp_ref, az_ref, cc_ref, sc_ref, cp_ref, sp_ref, o_ref):
    blk = pl.program_id(1)
    grp = A_HEADS // A_KV_HEADS
    q = _rope(q_ref[...], cc_ref[...], sc_ref[...]) * (A_HEAD_DIM ** -0.5)
    kvc = kvc_ref[...]
    kvp = kvp_ref[...]
    k_cur = _rope(kvc[:, :A_KV_WIDTH], cc_ref[...], sc_ref[...])
    k_prev = _rope(kvp[:, :A_KV_WIDTH], cp_ref[...], sp_ref[...])
    k_all = jnp.concatenate([k_prev, k_cur], axis=0).astype(BF16)
    v_all = jnp.concatenate([kvp[:, A_KV_WIDTH:], kvc[:, A_KV_WIDTH:]], axis=0).astype(BF16)
    qb = q.astype(BF16)

    rows = grp * A_BLOCK
    i = lax.broadcasted_iota(jnp.int32, (rows, 2 * A_BLOCK), 0) % A_BLOCK
    j = lax.broadcasted_iota(jnp.int32, (rows, 2 * A_BLOCK), 1)
    diff = i - j + A_BLOCK
    mask = (diff >= 0) & (diff < WINDOW) & ((j >= A_BLOCK) | (blk > 0))
    slab = lax.broadcasted_iota(jnp.int32, (rows, 1), 0) // A_BLOCK

    outs = []
    for g in range(A_KV_HEADS):
        kg = k_all[:, g * A_HEAD_DIM:(g + 1) * A_HEAD_DIM]
        vg = v_all[:, g * A_HEAD_DIM:(g + 1) * A_HEAD_DIM]
        qg = jnp.concatenate(
            [qb[:, (g * grp + h) * A_HEAD_DIM:(g * grp + h + 1) * A_HEAD_DIM] for h in range(grp)],
            axis=0)
        s = lax.dot_general(qg, kg, (((1,), (1,)), ((), ())), preferred_element_type=F32)
        s = jnp.where(mask, s, -jnp.inf)
        sink = jnp.zeros((rows, 1), F32)
        for h in range(grp):
            sink = jnp.where(slab == h, sinks_ref[g * grp + h], sink)
        m = jnp.maximum(jnp.max(s, axis=-1, keepdims=True), sink)
        p = jnp.exp(s - m)
        denom = jnp.sum(p, axis=-1, keepdims=True) + jnp.exp(sink - m)
        og = jnp.dot(p.astype(BF16), vg, preferred_element_type=F32) / denom
        for h in range(grp):
            outs.append(og[h * A_BLOCK:(h + 1) * A_BLOCK, :])
    out = jnp.concatenate(outs, axis=1)
    o_ref[...] = out * _silu(az_ref[...])


def _attention(proj, sinks, cos_t, sin_t, batch, seq):
    nb = seq // A_BLOCK
    T = batch * seq
    row = lambda b, i: b * nb + i
    prow = lambda b, i: b * nb + jnp.maximum(i - 1, 0)
    return pl.pallas_call(
        _attn_kernel,
        grid=(batch, nb),
        in_specs=[
            pl.BlockSpec(memory_space=pltpu.SMEM),
            pl.BlockSpec((A_BLOCK, A_WIDTH), lambda b, i: (row(b, i), COL_AQ // A_WIDTH)),
            pl.BlockSpec((A_BLOCK, 2 * A_KV_WIDTH), lambda b, i: (row(b, i), COL_KV // (2 * A_KV_WIDTH))),
            pl.BlockSpec((A_BLOCK, 2 * A_KV_WIDTH), lambda b, i: (prow(b, i), COL_KV // (2 * A_KV_WIDTH))),
            pl.BlockSpec((A_BLOCK, A_WIDTH), lambda b, i: (row(b, i), COL_AZ // A_WIDTH)),
            pl.BlockSpec((A_BLOCK, LANES), lambda b, i: (i, 0)),
            pl.BlockSpec((A_BLOCK, LANES), lambda b, i: (i, 0)),
            pl.BlockSpec((A_BLOCK, LANES), lambda b, i: (jnp.maximum(i - 1, 0), 0)),
            pl.BlockSpec((A_BLOCK, LANES), lambda b, i: (jnp.maximum(i - 1, 0), 0)),
        ],
        out_specs=pl.BlockSpec((A_BLOCK, A_WIDTH), lambda b, i: (row(b, i), 0)),
        out_shape=jax.ShapeDtypeStruct((T, A_WIDTH), F32),
        compiler_params=pltpu.CompilerParams(
            dimension_semantics=("parallel", "arbitrary"),
            vmem_limit_bytes=VMEM_LIMIT),
        name="swa_attention",
    )(sinks, proj, proj, proj, proj, cos_t, sin_t, cos_t, sin_t)


R_TB = 256


def _causal_conv(buf_ref, w, tb):
    acc = None
    for k in range(CONV_WIDTH):
        off = SUBLANES - (CONV_WIDTH - 1) + k
        term = buf_ref[off:off + tb, :] * w[k:k + 1, :]
        acc = term if acc is None else acc + term
    return acc


def _rglru_kernel(rx_ref, rz_ref, cw_ref, cb_ref, wa_ref, ba_ref, wx_ref, bx_ref, lam_ref,
                  o_ref, xbuf, acum, ucum, hcarry):
    tb = R_TB

    @pl.when(pl.program_id(1) == 0)
    def _():
        xbuf[0:SUBLANES, :] = jnp.zeros((SUBLANES, R_WIDTH), F32)
        hcarry[...] = jnp.zeros_like(hcarry)

    xbuf[SUBLANES:SUBLANES + tb, :] = rx_ref[...]
    xr = _causal_conv(xbuf, cw_ref[...], tb) + cb_ref[...]
    xbuf[0:SUBLANES, :] = xbuf[tb:tb + SUBLANES, :]

    xrb = xr.astype(BF16)
    ra, ia = [], []
    for n in range(R_BLOCKS):
        xb = xrb[:, n * R_BLOCK_DIM:(n + 1) * R_BLOCK_DIM]
        ra.append(jnp.dot(xb, wa_ref[n], preferred_element_type=F32))
        ia.append(jnp.dot(xb, wx_ref[n], preferred_element_type=F32))
    r = jax.nn.sigmoid(jnp.concatenate(ra, axis=1) + ba_ref[...])
    ig = jax.nn.sigmoid(jnp.concatenate(ia, axis=1) + bx_ref[...])
    log_a = (-R_C) * r * _softplus(-lam_ref[...])
    a = jnp.exp(log_a)
    u = jnp.sqrt(jnp.maximum(1.0 - a * a, 0.0)) * (ig * xr)

    row8 = lax.broadcasted_iota(jnp.int32, (tb, R_WIDTH), 0) % SUBLANES
    for k in (1, 2, 4):
        keep = row8 >= k
        a_sh = jnp.where(keep, pltpu.roll(a, k, 0), 1.0)
        u_sh = jnp.where(keep, pltpu.roll(u, k, 0), 0.0)
        u = a * u_sh + u
        a = a * a_sh
    acum[...] = a
    ucum[...] = u

    def body(g, hprev):
        r0 = pl.multiple_of(g * SUBLANES, SUBLANES)
        h = acum[pl.ds(r0, SUBLANES), :] * hprev + ucum[pl.ds(r0, SUBLANES), :]
        ucum[pl.ds(r0, SUBLANES), :] = h
        return jnp.broadcast_to(h[SUBLANES - 1:SUBLANES, :], (SUBLANES, R_WIDTH))

    hlast = lax.fori_loop(0, tb // SUBLANES, body, hcarry[...])
    hcarry[...] = hlast
    o_ref[...] = ucum[...] * _silu(rz_ref[...])


def _rglru(proj, cw, cb, wa, ba, wx, bx, lam, batch, seq):
    tb = R_TB
    nblk = seq // tb
    T = batch * seq
    vec = lambda: pl.BlockSpec((1, R_WIDTH), lambda b, i: (0, 0))
    return pl.pallas_call(
        _rglru_kernel,
        grid=(batch, nblk),
        in_specs=[
            pl.BlockSpec((tb, R_WIDTH), lambda b, i: (b * nblk + i, COL_RX // R_WIDTH)),
            pl.BlockSpec((tb, R_WIDTH), lambda b, i: (b * nblk + i, COL_RZ // R_WIDTH)),
            pl.BlockSpec((CONV_WIDTH, R_WIDTH), lambda b, i: (0, 0)),
            vec(),
            pl.BlockSpec((R_BLOCKS, R_BLOCK_DIM, R_BLOCK_DIM), lambda b, i: (0, 0, 0)),
            vec(),
            pl.BlockSpec((R_BLOCKS, R_BLOCK_DIM, R_BLOCK_DIM), lambda b, i: (0, 0, 0)),
            vec(),
            vec(),
        ],
        out_specs=pl.BlockSpec((tb, R_WIDTH), lambda b, i: (b * nblk + i, 0)),
        out_shape=jax.ShapeDtypeStruct((T, R_WIDTH), F32),
        scratch_shapes=[
            pltpu.VMEM((SUBLANES + tb, R_WIDTH), F32),
            pltpu.VMEM((tb, R_WIDTH), F32),
            pltpu.VMEM((tb, R_WIDTH), F32),
            pltpu.VMEM((SUBLANES, R_WIDTH), F32),
        ],
        compiler_params=pltpu.CompilerParams(
            dimension_semantics=("parallel", "arbitrary"),
            vmem_limit_bytes=VMEM_LIMIT),
        name="rglru",
    )(proj, proj, cw, cb, wa, ba, wx, bx, lam)


G_TB = 512


def _l2norm(x):
    return x * lax.rsqrt(jnp.sum(x * x, axis=-1, keepdims=True) + RMS_EPS)


def _unit_lower_inverse(m_strict):
    c = m_strict.shape[0]
    eye = (lax.broadcasted_iota(jnp.int32, (c, c), 0) ==
           lax.broadcasted_iota(jnp.int32, (c, c), 1)).astype(F32)
    a = -m_strict
    p = eye + a
    k = 2
    while k < c:
        a = jnp.dot(a, a, precision=HIGHEST, preferred_element_type=F32)
        p = p + jnp.dot(a, p, precision=HIGHEST, preferred_element_type=F32)
        k *= 2
    return p


def _gdn_kernel(gq_ref, gk_ref, gv_ref, gz_ref, sm_ref, cwq_ref, cwk_ref, cwv_ref,
                alog_ref, dtb_ref, nw_ref, o_ref,
                qbuf, kbuf, vbuf, qs, ks, vs, bg, state):
    tb = G_TB
    C = G_CHUNK

    @pl.when(pl.program_id(1) == 0)
    def _():
        zeros = jnp.zeros((SUBLANES, G_WIDTH), F32)
        qbuf[0:SUBLANES, :] = zeros
        kbuf[0:SUBLANES, :] = zeros
        vbuf[0:SUBLANES, :] = zeros
        state[...] = jnp.zeros_like(state)

    for src, buf, cw, dst in ((gq_ref, qbuf, cwq_ref, qs), (gk_ref, kbuf, cwk_ref, ks),
                              (gv_ref, vbuf, cwv_ref, vs)):
        buf[SUBLANES:SUBLANES + tb, :] = src[...]
        dst[...] = _silu(_causal_conv(buf, cw[...], tb))
        buf[0:SUBLANES, :] = buf[tb:tb + SUBLANES, :]

    small = sm_ref[...]
    lane = lax.broadcasted_iota(jnp.int32, (tb, LANES), 1)
    g_all = -jnp.exp(alog_ref[...]) * _softplus(small + dtb_ref[...])
    bg[...] = jnp.where(lane < G_HEADS, jax.nn.sigmoid(small), g_all)

    ri = lax.broadcasted_iota(jnp.int32, (C, C), 0)
    ci = lax.broadcasted_iota(jnp.int32, (C, C), 1)
    tril = ri >= ci
    strict = ri > ci
    tril_f = tril.astype(F32)
    triu_f = (ri <= ci).astype(F32)

    def chunk(c, carry):
        r0 = pl.multiple_of(c * C, C)
        bgc = bg[pl.ds(r0, C), :]
        gcum_col = jnp.dot(tril_f, bgc, precision=HIGHEST, preferred_element_type=F32)
        gcum_row = jnp.dot(bgc.T[:SUBLANES, :], triu_f, precision=HIGHEST,
                           preferred_element_type=F32)
        for h in range(G_HEADS):
            cols = slice(h * G_HEAD_DIM, (h + 1) * G_HEAD_DIM)
            qn = _l2norm(qs[pl.ds(r0, C), cols]) * (G_HEAD_DIM ** -0.5)
            kn = _l2norm(ks[pl.ds(r0, C), cols])
            v = vs[pl.ds(r0, C), cols]
            beta = bgc[:, h:h + 1]
            gc = gcum_col[:, G_HEADS + h:G_HEADS + h + 1]
            gr = gcum_row[G_HEADS + h:G_HEADS + h + 1, :]
            g_last = gc[C - 1:C, :]
            decay = jnp.where(tril, jnp.exp(gc - gr), 0.0)
            knb = kn.astype(BF16)
            qk_kk = lax.dot_general(jnp.concatenate([qn.astype(BF16), knb], axis=0), knb,
                                    (((1,), (1,)), ((), ())), preferred_element_type=F32)
            qk = qk_kk[:C] * decay
            m = jnp.where(strict, qk_kk[C:] * beta * decay, 0.0)
            tinv = _unit_lower_inverse(m)
            eg = jnp.exp(gc)
            kb = kn * beta
            rhs = jnp.concatenate([v * beta, kb * eg], axis=1)
            uw = jnp.dot(tinv, rhs, precision=HIGHEST, preferred_element_type=F32)
            u = uw[:, :G_HEAD_DIM]
            w = uw[:, G_HEAD_DIM:]
            s_old = state[h]
            sb = s_old.astype(BF16)
            v_new = u - jnp.dot(w.astype(BF16), sb, preferred_element_type=F32)
            vnb = v_new.astype(BF16)
            o = (jnp.dot((qn * eg).astype(BF16), sb, preferred_element_type=F32)
                 + jnp.dot(qk.astype(BF16), vnb, preferred_element_type=F32))
            k_dec = kn * jnp.exp(g_last - gc)
            state[h] = s_old * jnp.exp(g_last) + jnp.dot(k_dec.T.astype(BF16), vnb,
                                                         preferred_element_type=F32)
            on = o * lax.rsqrt(jnp.mean(o * o, axis=-1, keepdims=True) + RMS_EPS) * nw_ref[...]
            o_ref[pl.ds(r0, C), cols] = on * _silu(gz_ref[pl.ds(r0, C), cols])
        return carry

    lax.fori_loop(0, tb // C, chunk, 0)


def _gdn(proj, cwq, cwk, cwv, alog, dtb, nw, batch, seq):
    tb = G_TB
    nblk = seq // tb
    T = batch * seq
    seg = lambda col: pl.BlockSpec((tb, G_WIDTH), lambda b, i: (b * nblk + i, col // G_WIDTH))
    cw = lambda: pl.BlockSpec((CONV_WIDTH, G_WIDTH), lambda b, i: (0, 0))
    vec = lambda: pl.BlockSpec((1, LANES), lambda b, i: (0, 0))
    return pl.pallas_call(
        _gdn_kernel,
        grid=(batch, nblk),
        in_specs=[seg(COL_GQ), seg(COL_GK), seg(COL_GV), seg(COL_GZ),
                  pl.BlockSpec((tb, LANES), lambda b, i: (b * nblk + i, COL_SMALL // LANES)),
                  cw(), cw(), cw(), vec(), vec(), vec()],
        out_specs=pl.BlockSpec((tb, G_WIDTH), lambda b, i: (b * nblk + i, 0)),
        out_shape=jax.ShapeDtypeStruct((T, G_WIDTH), F32),
        scratch_shapes=[
            pltpu.VMEM((SUBLANES + tb, G_WIDTH), F32),
            pltpu.VMEM((SUBLANES + tb, G_WIDTH), F32),
            pltpu.VMEM((SUBLANES + tb, G_WIDTH), F32),
            pltpu.VMEM((tb, G_WIDTH), F32),
            pltpu.VMEM((tb, G_WIDTH), F32),
            pltpu.VMEM((tb, G_WIDTH), F32),
            pltpu.VMEM((tb, LANES), F32),
            pltpu.VMEM((G_HEADS, G_HEAD_DIM, G_HEAD_DIM), F32),
        ],
        compiler_params=pltpu.CompilerParams(
            dimension_semantics=("parallel", "arbitrary"),
            vmem_limit_bytes=VMEM_LIMIT),
        name="gated_deltanet",
    )(proj, proj, proj, proj, proj, cwq, cwk, cwv, alog, dtb, nw)


def _outproj_kernel(ya_ref, yr_ref, yg_ref, x_ref, w_ref, g_ref, b_ref, o_ref, ob_ref):
    y = jnp.dot(ya_ref[...].astype(BF16), w_ref[0:A_WIDTH, :], preferred_element_type=F32)
    y += jnp.dot(yr_ref[...].astype(BF16), w_ref[A_WIDTH:A_WIDTH + R_WIDTH, :],
                 preferred_element_type=F32)
    y += jnp.dot(yg_ref[...].astype(BF16), w_ref[A_WIDTH + R_WIDTH:MIX_WIDTH, :],
                 preferred_element_type=F32)
    z = DEEPNORM_ALPHA * x_ref[...] + y
    mu = jnp.mean(z, axis=-1, keepdims=True)
    zc = z - mu
    var = jnp.mean(zc * zc, axis=-1, keepdims=True)
    out = zc * lax.rsqrt(var + LN_EPS) * g_ref[...] + b_ref[...]
    o_ref[...] = out
    ob_ref[...] = out.astype(BF16)


def _outproj(ya, yr, yg, x, w, g, b):
    T = x.shape[0]
    tm = 256
    row = lambda width: pl.BlockSpec((tm, width), lambda i: (i, 0))
    vec = lambda: pl.BlockSpec((1, D_MODEL), lambda i: (0, 0))
    return pl.pallas_call(
        _outproj_kernel,
        grid=(T // tm,),
        in_specs=[row(A_WIDTH), row(R_WIDTH), row(G_WIDTH), row(D_MODEL),
                  pl.BlockSpec((MIX_WIDTH, D_MODEL), lambda i: (0, 0)), vec(), vec()],
        out_specs=[row(D_MODEL), row(D_MODEL)],
        out_shape=[jax.ShapeDtypeStruct((T, D_MODEL), F32),
                   jax.ShapeDtypeStruct((T, D_MODEL), BF16)],
        compiler_params=pltpu.CompilerParams(
            dimension_semantics=("parallel",),
            vmem_limit_bytes=VMEM_LIMIT),
        name="outproj_deepnorm",
    )(ya, yr, yg, x, w, g, b)


def _reorder_w_in(w_in):
    sizes = (A_WIDTH, A_KV_WIDTH, A_KV_WIDTH, A_WIDTH, R_WIDTH, R_WIDTH,
             G_WIDTH, G_WIDTH, G_WIDTH, G_WIDTH, G_HEADS, G_HEADS)
    pts = np.cumsum((0,) + sizes)
    aq, ak, av, az, rx, rz, gq, gk, gv, gz, gb, ga = (
        w_in[..., int(pts[n]):int(pts[n + 1])] for n in range(len(sizes)))
    used = COL_SMALL + 2 * G_HEADS
    pad = jnp.zeros(w_in.shape[:-1] + (N_PROJ - used,), w_in.dtype)
    return jnp.concatenate([rx, rz, aq, az, gq, gk, gv, gz, ak, av, gb, ga, pad], axis=-1).astype(BF16)


def _rope_tables(seq):
    half = A_HEAD_DIM // 2
    inv = 1.0 / (ROPE_THETA ** (jnp.arange(0, A_HEAD_DIM, 2, dtype=F32) / A_HEAD_DIM))
    ang = jnp.arange(seq, dtype=F32)[:, None] * inv[None, :]
    cos, sin = jnp.cos(ang), jnp.sin(ang)
    reps = LANES // A_HEAD_DIM
    cos_t = jnp.tile(jnp.concatenate([cos, cos], axis=1), (1, reps))
    sin_t = jnp.tile(jnp.concatenate([-sin, sin], axis=1), (1, reps))
    assert cos_t.shape == (seq, LANES) and half * 2 == A_HEAD_DIM
    return cos_t, sin_t


def _lane_vec(v, offset):
    return jnp.zeros((1, LANES), F32).at[0, offset:offset + v.shape[0]].set(v.astype(F32))


@jax.jit
def _forward(x, w_in, sinks, r_conv_w, r_conv_b, r_wa, r_ba, r_wx, r_bx, r_lam,
             g_conv_w, g_a_log, g_dt_bias, g_norm_w, w_out, ln_g, ln_b):
    batch, seq, _ = x.shape
    T = batch * seq
    cos_t, sin_t = _rope_tables(seq)
    w_in_r = _reorder_w_in(w_in)
    w_out_b = w_out.astype(BF16)
    xf = x.reshape(T, D_MODEL)
    xb = xf.astype(BF16)
    for l in range(DEPTH):
        proj = _inproj(xb, w_in_r[l])
        ya = _attention(proj, sinks[l], cos_t, sin_t, batch, seq)
        yr = _rglru(proj, r_conv_w[l], r_conv_b[l][None, :], r_wa[l].astype(BF16), r_ba[l][None, :],
                    r_wx[l].astype(BF16), r_bx[l][None, :], r_lam[l][None, :], batch, seq)
        gcw = g_conv_w[l]
        yg = _gdn(proj, gcw[:, :G_WIDTH], gcw[:, G_WIDTH:2 * G_WIDTH], gcw[:, 2 * G_WIDTH:],
                  _lane_vec(g_a_log[l], G_HEADS), _lane_vec(g_dt_bias[l], G_HEADS),
                  g_norm_w[l][None, :], batch, seq)
        xf, xb = _outproj(ya, yr, yg, xf, w_out_b[l], ln_g[l][None, :], ln_b[l][None, :])
    return xf.reshape(batch, seq, D_MODEL)


def kernel(x, w_in, sinks, r_conv_w, r_conv_b, r_wa, r_ba, r_wx, r_bx, r_lam, g_conv_w, g_a_log,
           g_dt_bias, g_norm_w, w_out, ln_g, ln_b):
    return _forward(x, w_in, sinks, r_conv_w, r_conv_b, r_wa, r_ba, r_wx, r_bx, r_lam,
                    g_conv_w, g_a_log, g_dt_bias, g_norm_w, w_out, ln_g, ln_b)
```

```python
import functools
import math

import numpy as np
import jax
import jax.numpy as jnp
from jax import lax
from jax.experimental import pallas as pl
from jax.experimental.pallas import tpu as pltpu

D_MODEL = 2048
DEPTH = 2
A_HEADS = 8
A_KV_HEADS = 2
A_HEAD_DIM = 64
A_WIDTH = A_HEADS * A_HEAD_DIM
A_KV_WIDTH = A_KV_HEADS * A_HEAD_DIM
WINDOW = 128
A_BLOCK = 128
ROPE_THETA = 10000.0
R_WIDTH = 1024
R_BLOCKS = 8
R_BLOCK_DIM = R_WIDTH // R_BLOCKS
R_C = 8.0
CONV_WIDTH = 4
G_HEADS = 4
G_HEAD_DIM = 128
G_WIDTH = G_HEADS * G_HEAD_DIM
G_CHUNK = 64
MIX_WIDTH = A_WIDTH + R_WIDTH + G_WIDTH
DEEPNORM_ALPHA = (2 * DEPTH) ** 0.25
LN_EPS = 1e-5
RMS_EPS = 1e-6

LANES = 128
SUBLANES = 8
VMEM_LIMIT = 56 * 1024 * 1024

COL_RX = 0
COL_RZ = 1024
COL_AQ = 2048
COL_AZ = 2560
COL_GQ = 3072
COL_GK = 3584
COL_GV = 4096
COL_GZ = 4608
COL_KV = 5120
COL_SMALL = 5376
N_PROJ = 5632

F32 = jnp.float32
BF16 = jnp.bfloat16
HIGHEST = lax.Precision.HIGHEST


def _silu(x):
    return x * jax.nn.sigmoid(x)


def _softplus(x):
    return jnp.maximum(x, 0.0) + jnp.log1p(jnp.exp(-jnp.abs(x)))


def _inproj_kernel(x_ref, w_ref, o_ref):
    o_ref[...] = jnp.dot(x_ref[...], w_ref[...], preferred_element_type=F32)


def _inproj(xb, w):
    T = xb.shape[0]
    tm, tn = 1024, 512
    return pl.pallas_call(
        _inproj_kernel,
        grid=(T // tm, N_PROJ // tn),
        in_specs=[pl.BlockSpec((tm, D_MODEL), lambda i, j: (i, 0)),
                  pl.BlockSpec((D_MODEL, tn), lambda i, j: (0, j))],
        out_specs=pl.BlockSpec((tm, tn), lambda i, j: (i, j)),
        out_shape=jax.ShapeDtypeStruct((T, N_PROJ), F32),
        compiler_params=pltpu.CompilerParams(
            dimension_semantics=("parallel", "arbitrary"),
            vmem_limit_bytes=VMEM_LIMIT),
        name="inproj",
    )(xb, w)


def _rope(x, cos, sin_signed):
    w = x.shape[1]
    reps = w // LANES
    if reps > 1:
        cos = jnp.concatenate([cos] * reps, axis=1)
        sin_signed = jnp.concatenate([sin_signed] * reps, axis=1)
    lane = lax.broadcasted_iota(jnp.int32, x.shape, 1)
    first_half = (lane % A_HEAD_DIM) < (A_HEAD_DIM // 2)
    swapped = jnp.where(first_half,
                        pltpu.roll(x, w - A_HEAD_DIM // 2, 1),
                        pltpu.roll(x, A_HEAD_DIM // 2, 1))
    return x * cos + swapped * sin_signed


def _attn_kernel(sinks_ref, q_ref, kvc_ref, kvp_ref, az_ref, cc_ref, sc_ref, cp_ref, sp_ref, o_ref):
    blk = pl.program_id(1)
    grp = A_HEADS // A_KV_HEADS
    q = _rope(q_ref[...], cc_ref[...], sc_ref[...]) * (A_HEAD_DIM ** -0.5)
    kvc = kvc_ref[...]
    kvp = kvp_ref[...]
    k_cur = _rope(kvc[:, :A_KV_WIDTH], cc_ref[...], sc_ref[...])
    k_prev = _rope(kvp[:, :A_KV_WIDTH], cp_ref[...], sp_ref[...])
    k_all = jnp.concatenate([k_prev, k_cur], axis=0).astype(BF16)
    v_all = jnp.concatenate([kvp[:, A_KV_WIDTH:], kvc[:, A_KV_WIDTH:]], axis=0).astype(BF16)
    qb = q.astype(BF16)

    rows = grp * A_BLOCK
    i = lax.broadcasted_iota(jnp.int32, (rows, 2 * A_BLOCK), 0) % A_BLOCK
    j = lax.broadcasted_iota(jnp.int32, (rows, 2 * A_BLOCK), 1)
    diff = i - j + A_BLOCK
    mask = (diff >= 0) & (diff < WINDOW) & ((j >= A_BLOCK) | (blk > 0))
    slab = lax.broadcasted_iota(jnp.int32, (rows, 1), 0) // A_BLOCK

    outs = []
    for g in range(A_KV_HEADS):
        kg = k_all[:, g * A_HEAD_DIM:(g + 1) * A_HEAD_DIM]
        vg = v_all[:, g * A_HEAD_DIM:(g + 1) * A_HEAD_DIM]
        qg = jnp.concatenate(
            [qb[:, (g * grp + h) * A_HEAD_DIM:(g * grp + h + 1) * A_HEAD_DIM] for h in range(grp)],
            axis=0)
        s = lax.dot_general(qg, kg, (((1,), (1,)), ((), ())), preferred_element_type=F32)
        s = jnp.where(mask, s, -jnp.inf)
        sink = jnp.zeros((rows, 1), F32)
        for h in range(grp):
            sink = jnp.where(slab == h, sinks_ref[g * grp + h], sink)
        m = jnp.maximum(jnp.max(s, axis=-1, keepdims=True), sink)
        p = jnp.exp(s - m)
        denom = jnp.sum(p, axis=-1, keepdims=True) + jnp.exp(sink - m)
        og = jnp.dot(p.astype(BF16), vg, preferred_element_type=F32) / denom
        for h in range(grp):
            outs.append(og[h * A_BLOCK:(h + 1) * A_BLOCK, :])
    out = jnp.concatenate(outs, axis=1)
    o_ref[...] = out * _silu(az_ref[...])


def _attention(proj, sinks, cos_t, sin_t, batch, seq):
    nb = seq // A_BLOCK
    T = batch * seq
    row = lambda b, i: b * nb + i
    prow = lambda b, i: b * nb + jnp.maximum(i - 1, 0)
    return pl.pallas_call(
        _attn_kernel,
        grid=(batch, nb),
        in_specs=[
            pl.BlockSpec(memory_space=pltpu.SMEM),
            pl.BlockSpec((A_BLOCK, A_WIDTH), lambda b, i: (row(b, i), COL_AQ // A_WIDTH)),
            pl.BlockSpec((A_BLOCK, 2 * A_KV_WIDTH), lambda b, i: (row(b, i), COL_KV // (2 * A_KV_WIDTH))),
            pl.BlockSpec((A_BLOCK, 2 * A_KV_WIDTH), lambda b, i: (prow(b, i), COL_KV // (2 * A_KV_WIDTH))),
            pl.BlockSpec((A_BLOCK, A_WIDTH), lambda b, i: (row(b, i), COL_AZ // A_WIDTH)),
            pl.BlockSpec((A_BLOCK, LANES), lambda b, i: (i, 0)),
            pl.BlockSpec((A_BLOCK, LANES), lambda b, i: (i, 0)),
            pl.BlockSpec((A_BLOCK, LANES), lambda b, i: (jnp.maximum(i - 1, 0), 0)),
            pl.BlockSpec((A_BLOCK, LANES), lambda b, i: (jnp.maximum(i - 1, 0), 0)),
        ],
        out_specs=pl.BlockSpec((A_BLOCK, A_WIDTH), lambda b, i: (row(b, i), 0)),
        out_shape=jax.ShapeDtypeStruct((T, A_WIDTH), F32),
        compiler_params=pltpu.CompilerParams(
            dimension_semantics=("parallel", "arbitrary"),
            vmem_limit_bytes=VMEM_LIMIT),
        name="swa_attention",
    )(sinks, proj, proj, proj, proj, cos_t, sin_t, cos_t, sin_t)


R_TB = 256


def _causal_conv(buf_ref, w, tb):
    acc = None
    for k in range(CONV_WIDTH):
        off = SUBLANES - (CONV_WIDTH - 1) + k
        term = buf_ref[off:off + tb, :] * w[k:k + 1, :]
        acc = term if acc is None else acc + term
    return acc


def _rglru_kernel(rx_ref, rz_ref, cw_ref, cb_ref, wa_ref, ba_ref, wx_ref, bx_ref, lam_ref,
                  o_ref, xbuf, acum, ucum, hcarry):
    tb = R_TB

    @pl.when(pl.program_id(1) == 0)
    def _():
        xbuf[0:SUBLANES, :] = jnp.zeros((SUBLANES, R_WIDTH), F32)
        hcarry[...] = jnp.zeros_like(hcarry)

    xbuf[SUBLANES:SUBLANES + tb, :] = rx_ref[...]
    xr = _causal_conv(xbuf, cw_ref[...], tb) + cb_ref[...]
    xbuf[0:SUBLANES, :] = xbuf[tb:tb + SUBLANES, :]

    xrb = xr.astype(BF16)
    ra, ia = [], []
    for n in range(R_BLOCKS):
        xb = xrb[:, n * R_BLOCK_DIM:(n + 1) * R_BLOCK_DIM]
        ra.append(jnp.dot(xb, wa_ref[n], preferred_element_type=F32))
        ia.append(jnp.dot(xb, wx_ref[n], preferred_element_type=F32))
    r = jax.nn.sigmoid(jnp.concatenate(ra, axis=1) + ba_ref[...])
    ig = jax.nn.sigmoid(jnp.concatenate(ia, axis=1) + bx_ref[...])
    log_a = (-R_C) * r * _softplus(-lam_ref[...])
    a = jnp.exp(log_a)
    u = jnp.sqrt(jnp.maximum(1.0 - a * a, 0.0)) * (ig * xr)

    row8 = lax.broadcasted_iota(jnp.int32, (tb, R_WIDTH), 0) % SUBLANES
    for k in (1, 2, 4):
        keep = row8 >= k
        a_sh = jnp.where(keep, pltpu.roll(a, k, 0), 1.0)
        u_sh = jnp.where(keep, pltpu.roll(u, k, 0), 0.0)
        u = a * u_sh + u
        a = a * a_sh
    acum[...] = a
    ucum[...] = u

    def body(g, hprev):
        r0 = pl.multiple_of(g * SUBLANES, SUBLANES)
        h = acum[pl.ds(r0, SUBLANES), :] * hprev + ucum[pl.ds(r0, SUBLANES), :]
        ucum[pl.ds(r0, SUBLANES), :] = h
        return jnp.broadcast_to(h[SUBLANES - 1:SUBLANES, :], (SUBLANES, R_WIDTH))

    hlast = lax.fori_loop(0, tb // SUBLANES, body, hcarry[...])
    hcarry[...] = hlast
    o_ref[...] = ucum[...] * _silu(rz_ref[...])


def _rglru(proj, cw, cb, wa, ba, wx, bx, lam, batch, seq):
    tb = R_TB
    nblk = seq // tb
    T = batch * seq
    vec = lambda: pl.BlockSpec((1, R_WIDTH), lambda b, i: (0, 0))
    return pl.pallas_call(
        _rglru_kernel,
        grid=(batch, nblk),
        in_specs=[
            pl.BlockSpec((tb, R_WIDTH), lambda b, i: (b * nblk + i, COL_RX // R_WIDTH)),
            pl.BlockSpec((tb, R_WIDTH), lambda b, i: (b * nblk + i, COL_RZ // R_WIDTH)),
            pl.BlockSpec((CONV_WIDTH, R_WIDTH), lambda b, i: (0, 0)),
            vec(),
            pl.BlockSpec((R_BLOCKS, R_BLOCK_DIM, R_BLOCK_DIM), lambda b, i: (0, 0, 0)),
            vec(),
            pl.BlockSpec((R_BLOCKS, R_BLOCK_DIM, R_BLOCK_DIM), lambda b, i: (0, 0, 0)),
            vec(),
            vec(),
        ],
        out_specs=pl.BlockSpec((tb, R_WIDTH), lambda b, i: (b * nblk + i, 0)),
        out_shape=jax.ShapeDtypeStruct((T, R_WIDTH), F32),
        scratch_shapes=[
            pltpu.VMEM((SUBLANES + tb, R_WIDTH), F32),
            pltpu.VMEM((tb, R_WIDTH), F32),
            pltpu.VMEM((tb, R_WIDTH), F32),
            pltpu.VMEM((SUBLANES, R_WIDTH), F32),
        ],
        compiler_params=pltpu.CompilerParams(
            dimension_semantics=("parallel", "arbitrary"),
            vmem_limit_bytes=VMEM_LIMIT),
        name="rglru",
    )(proj, proj, cw, cb, wa, ba, wx, bx, lam)


G_TB = 512


def _l2norm(x):
    return x * lax.rsqrt(jnp.sum(x * x, axis=-1, keepdims=True) + RMS_EPS)


def _bdot(a, b):
    return jnp.dot(a.astype(BF16), b.astype(BF16), preferred_element_type=F32)


def _unit_lower_inverses(ms):
    c = ms[0].shape[0]
    eye = (lax.broadcasted_iota(jnp.int32, (c, c), 0) ==
           lax.broadcasted_iota(jnp.int32, (c, c), 1)).astype(F32)
    a_s = [-m for m in ms]
    p_s = [eye + a for a in a_s]
    k = 2
    while k < c:
        a_s = [_bdot(a, a) for a in a_s]
        p_s = [p + _bdot(a, p) for a, p in zip(a_s, p_s)]
        k *= 2
    return p_s


G_GROUP = 2


def _gdn_kernel(gq_ref, gk_ref, gv_ref, gz_ref, sm_ref, cwq_ref, cwk_ref, cwv_ref,
                alog_ref, dtb_ref, nw_ref, o_ref,
                qbuf, kbuf, vbuf, qs, ks, vs, bg, us, ws, qds, qks, kdts, gls, state):
    tb = G_TB
    C = G_CHUNK

    @pl.when(pl.program_id(1) == 0)
    def _():
        zeros = jnp.zeros((SUBLANES, G_WIDTH), F32)
        qbuf[0:SUBLANES, :] = zeros
        kbuf[0:SUBLANES, :] = zeros
        vbuf[0:SUBLANES, :] = zeros
        state[...] = jnp.zeros_like(state)

    for src, buf, cw, dst in ((gq_ref, qbuf, cwq_ref, qs), (gk_ref, kbuf, cwk_ref, ks),
                              (gv_ref, vbuf, cwv_ref, vs)):
        buf[SUBLANES:SUBLANES + tb, :] = src[...]
        dst[...] = _silu(_causal_conv(buf, cw[...], tb))
        buf[0:SUBLANES, :] = buf[tb:tb + SUBLANES, :]

    small = sm_ref[...]
    lane = lax.broadcasted_iota(jnp.int32, (tb, LANES), 1)
    g_all = -jnp.exp(alog_ref[...]) * _softplus(small + dtb_ref[...])
    bg[...] = jnp.where(lane < G_HEADS, jax.nn.sigmoid(small), g_all)

    ri = lax.broadcasted_iota(jnp.int32, (C, C), 0)
    ci = lax.broadcasted_iota(jnp.int32, (C, C), 1)
    tril = ri >= ci
    strict = ri > ci
    tril_f = tril.astype(F32)

    def intra(grp, carry):
        pairs, ms = [], []
        for cc in range(G_GROUP):
            c = grp * G_GROUP + cc
            r0 = pl.multiple_of(c * C, C)
            bgc = bg[pl.ds(r0, C), :]
            gcum_col = jnp.dot(tril_f, bgc, precision=HIGHEST, preferred_element_type=F32)
            gcum_row = gcum_col.T[:SUBLANES, :]
            for h in range(G_HEADS):
                cols = slice(h * G_HEAD_DIM, (h + 1) * G_HEAD_DIM)
                qn = _l2norm(qs[pl.ds(r0, C), cols]) * (G_HEAD_DIM ** -0.5)
                kn = _l2norm(ks[pl.ds(r0, C), cols])
                v = vs[pl.ds(r0, C), cols]
                beta = bgc[:, h:h + 1]
                gc = gcum_col[:, G_HEADS + h:G_HEADS + h + 1]
                gr = gcum_row[G_HEADS + h:G_HEADS + h + 1, :]
                g_last = gc[C - 1:C, :]
                decay = jnp.where(tril, jnp.exp(gc - gr), 0.0)
                knb = kn.astype(BF16)
                qk_kk = lax.dot_general(jnp.concatenate([qn.astype(BF16), knb], axis=0), knb,
                                        (((1,), (1,)), ((), ())), preferred_element_type=F32)
                qks[pl.ds(r0, C), h * C:(h + 1) * C] = (qk_kk[:C] * decay).astype(BF16)
                ms.append(jnp.where(strict, qk_kk[C:] * beta * decay, 0.0))
                eg = jnp.exp(gc)
                qds[pl.ds(r0, C), cols] = (qn * eg).astype(BF16)
                kdts[c * G_HEADS + h] = (kn * jnp.exp(g_last - gc)).T.astype(BF16)
                gls[c * G_HEADS + h] = jnp.broadcast_to(jnp.exp(g_last), (SUBLANES, LANES))
                pairs.append((r0, cols, jnp.concatenate([v * beta, kn * (beta * eg)], axis=1)))
        tinvs = _unit_lower_inverses(ms)
        for (r0, cols, rhs), tinv in zip(pairs, tinvs):
            uw = _bdot(tinv, rhs)
            us[pl.ds(r0, C), cols] = uw[:, :G_HEAD_DIM]
            ws[pl.ds(r0, C), cols] = uw[:, G_HEAD_DIM:].astype(BF16)
        return carry

    lax.fori_loop(0, tb // (C * G_GROUP), intra, 0)

    def inter(c, carry):
        r0 = pl.multiple_of(c * C, C)
        for h in range(G_HEADS):
            cols = slice(h * G_HEAD_DIM, (h + 1) * G_HEAD_DIM)
            s_old = state[h]
            sb = s_old.astype(BF16)
            v_new = us[pl.ds(r0, C), cols] - jnp.dot(ws[pl.ds(r0, C), cols], sb,
                                                      preferred_element_type=F32)
            vnb = v_new.astype(BF16)
            o = (jnp.dot(qds[pl.ds(r0, C), cols], sb, preferred_element_type=F32)
                 + jnp.dot(qks[pl.ds(r0, C), h * C:(h + 1) * C], vnb, preferred_element_type=F32))
            gl = gls[c * G_HEADS + h][0:1, :]
            state[h] = s_old * gl + jnp.dot(kdts[c * G_HEADS + h], vnb, preferred_element_type=F32)
            on = o * lax.rsqrt(jnp.mean(o * o, axis=-1, keepdims=True) + RMS_EPS) * nw_ref[...]
            o_ref[pl.ds(r0, C), cols] = on * _silu(gz_ref[pl.ds(r0, C), cols])
        return carry

    lax.fori_loop(0, tb // C, inter, 0)


def _gdn(proj, cwq, cwk, cwv, alog, dtb, nw, batch, seq):
    tb = G_TB
    nblk = seq // tb
    T = batch * seq
    seg = lambda col: pl.BlockSpec((tb, G_WIDTH), lambda b, i: (b * nblk + i, col // G_WIDTH))
    cw = lambda: pl.BlockSpec((CONV_WIDTH, G_WIDTH), lambda b, i: (0, 0))
    vec = lambda: pl.BlockSpec((1, LANES), lambda b, i: (0, 0))
    return pl.pallas_call(
        _gdn_kernel,
        grid=(batch, nblk),
        in_specs=[seg(COL_GQ), seg(COL_GK), seg(COL_GV), seg(COL_GZ),
                  pl.BlockSpec((tb, LANES), lambda b, i: (b * nblk + i, COL_SMALL // LANES)),
                  cw(), cw(), cw(), vec(), vec(), vec()],
        out_specs=pl.BlockSpec((tb, G_WIDTH), lambda b, i: (b * nblk + i, 0)),
        out_shape=jax.ShapeDtypeStruct((T, G_WIDTH), F32),
        scratch_shapes=[
            pltpu.VMEM((SUBLANES + tb, G_WIDTH), F32),
            pltpu.VMEM((SUBLANES + tb, G_WIDTH), F32),
            pltpu.VMEM((SUBLANES + tb, G_WIDTH), F32),
            pltpu.VMEM((tb, G_WIDTH), F32),
            pltpu.VMEM((tb, G_WIDTH), F32),
            pltpu.VMEM((tb, G_WIDTH), F32),
            pltpu.VMEM((tb, LANES), F32),
            pltpu.VMEM((tb, G_WIDTH), F32),
            pltpu.VMEM((tb, G_WIDTH), BF16),
            pltpu.VMEM((tb, G_WIDTH), BF16),
            pltpu.VMEM((tb, G_HEADS * G_CHUNK), BF16),
            pltpu.VMEM((tb // G_CHUNK * G_HEADS, G_HEAD_DIM, G_CHUNK), BF16),
            pltpu.VMEM((tb // G_CHUNK * G_HEADS, SUBLANES, LANES), F32),
            pltpu.VMEM((G_HEADS, G_HEAD_DIM, G_HEAD_DIM), F32),
        ],
        compiler_params=pltpu.CompilerParams(
            dimension_semantics=("parallel", "arbitrary"),
            vmem_limit_bytes=VMEM_LIMIT),
        name="gated_deltanet",
    )(proj, proj, proj, proj, proj, cwq, cwk, cwv, alog, dtb, nw)


def _outproj_kernel(ya_ref, yr_ref, yg_ref, x_ref, w_ref, g_ref, b_ref, o_ref, ob_ref):
    y = jnp.dot(ya_ref[...].astype(BF16), w_ref[0:A_WIDTH, :], preferred_element_type=F32)
    y += jnp.dot(yr_ref[...].astype(BF16), w_ref[A_WIDTH:A_WIDTH + R_WIDTH, :],
                 preferred_element_type=F32)
    y += jnp.dot(yg_ref[...].astype(BF16), w_ref[A_WIDTH + R_WIDTH:MIX_WIDTH, :],
                 preferred_element_type=F32)
    z = DEEPNORM_ALPHA * x_ref[...] + y
    mu = jnp.mean(z, axis=-1, keepdims=True)
    zc = z - mu
    var = jnp.mean(zc * zc, axis=-1, keepdims=True)
    out = zc * lax.rsqrt(var + LN_EPS) * g_ref[...] + b_ref[...]
    o_ref[...] = out
    ob_ref[...] = out.astype(BF16)


def _outproj(ya, yr, yg, x, w, g, b):
    T = x.shape[0]
    tm = 256
    row = lambda width: pl.BlockSpec((tm, width), lambda i: (i, 0))
    vec = lambda: pl.BlockSpec((1, D_MODEL), lambda i: (0, 0))
    return pl.pallas_call(
        _outproj_kernel,
        grid=(T // tm,),
        in_specs=[row(A_WIDTH), row(R_WIDTH), row(G_WIDTH), row(D_MODEL),
                  pl.BlockSpec((MIX_WIDTH, D_MODEL), lambda i: (0, 0)), vec(), vec()],
        out_specs=[row(D_MODEL), row(D_MODEL)],
        out_shape=[jax.ShapeDtypeStruct((T, D_MODEL), F32),
                   jax.ShapeDtypeStruct((T, D_MODEL), BF16)],
        compiler_params=pltpu.CompilerParams(
            dimension_semantics=("parallel",),
            vmem_limit_bytes=VMEM_LIMIT),
        name="outproj_deepnorm",
    )(ya, yr, yg, x, w, g, b)


def _reorder_w_in(w_in):
    sizes = (A_WIDTH, A_KV_WIDTH, A_KV_WIDTH, A_WIDTH, R_WIDTH, R_WIDTH,
             G_WIDTH, G_WIDTH, G_WIDTH, G_WIDTH, G_HEADS, G_HEADS)
    pts = np.cumsum((0,) + sizes)
    aq, ak, av, az, rx, rz, gq, gk, gv, gz, gb, ga = (
        w_in[..., int(pts[n]):int(pts[n + 1])] for n in range(len(sizes)))
    used = COL_SMALL + 2 * G_HEADS
    pad = jnp.zeros(w_in.shape[:-1] + (N_PROJ - used,), w_in.dtype)
    return jnp.concatenate([rx, rz, aq, az, gq, gk, gv, gz, ak, av, gb, ga, pad], axis=-1).astype(BF16)


def _rope_tables(seq):
    half = A_HEAD_DIM // 2
    inv = 1.0 / (ROPE_THETA ** (jnp.arange(0, A_HEAD_DIM, 2, dtype=F32) / A_HEAD_DIM))
    ang = jnp.arange(seq, dtype=F32)[:, None] * inv[None, :]
    cos, sin = jnp.cos(ang), jnp.sin(ang)
    reps = LANES // A_HEAD_DIM
    cos_t = jnp.tile(jnp.concatenate([cos, cos], axis=1), (1, reps))
    sin_t = jnp.tile(jnp.concatenate([-sin, sin], axis=1), (1, reps))
    assert cos_t.shape == (seq, LANES) and half * 2 == A_HEAD_DIM
    return cos_t, sin_t


def _lane_vec(v, offset):
    return jnp.zeros((1, LANES), F32).at[0, offset:offset + v.shape[0]].set(v.astype(F32))


@jax.jit
def _forward(x, w_in, sinks, r_conv_w, r_conv_b, r_wa, r_ba, r_wx, r_bx, r_lam,
             g_conv_w, g_a_log, g_dt_bias, g_norm_w, w_out, ln_g, ln_b):
    batch, seq, _ = x.shape
    T = batch * seq
    cos_t, sin_t = _rope_tables(seq)
    w_in_r = _reorder_w_in(w_in)
    w_out_b = w_out.astype(BF16)
    xf = x.reshape(T, D_MODEL)
    xb = xf.astype(BF16)
    for l in range(DEPTH):
        proj = _inproj(xb, w_in_r[l])
        ya = _attention(proj, sinks[l], cos_t, sin_t, batch, seq)
        yr = _rglru(proj, r_conv_w[l], r_conv_b[l][None, :], r_wa[l].astype(BF16), r_ba[l][None, :],
                    r_wx[l].astype(BF16), r_bx[l][None, :], r_lam[l][None, :], batch, seq)
        gcw = g_conv_w[l]
        yg = _gdn(proj, gcw[:, :G_WIDTH], gcw[:, G_WIDTH:2 * G_WIDTH], gcw[:, 2 * G_WIDTH:],
                  _lane_vec(g_a_log[l], G_HEADS), _lane_vec(g_dt_bias[l], G_HEADS),
                  g_norm_w[l][None, :], batch, seq)
        xf, xb = _outproj(ya, yr, yg, xf, w_out_b[l], ln_g[l][None, :], ln_b[l][None, :])
    return xf.reshape(batch, seq, D_MODEL)


def kernel(x, w_in, sinks, r_conv_w, r_conv_b, r_wa, r_ba, r_wx, r_bx, r_lam, g_conv_w, g_a_log,
           g_dt_bias, g_norm_w, w_out, ln_g, ln_b):
    return _forward(x, w_in, sinks, r_conv_w, r_conv_b, r_wa, r_ba, r_wx, r_bx, r_lam,
                    g_conv_w, g_a_log, g_dt_bias, g_norm_w, w_out, ln_g, ln_b)
```

```python
import functools
import math

import numpy as np
import jax
import jax.numpy as jnp
from jax import lax
from jax.experimental import pallas as pl
from jax.experimental.pallas import tpu as pltpu

D_MODEL = 2048
DEPTH = 2
A_HEADS = 8
A_KV_HEADS = 2
A_HEAD_DIM = 64
A_WIDTH = A_HEADS * A_HEAD_DIM
A_KV_WIDTH = A_KV_HEADS * A_HEAD_DIM
WINDOW = 128
A_BLOCK = 128
ROPE_THETA = 10000.0
R_WIDTH = 1024
R_BLOCKS = 8
R_BLOCK_DIM = R_WIDTH // R_BLOCKS
R_C = 8.0
CONV_WIDTH = 4
G_HEADS = 4
G_HEAD_DIM = 128
G_WIDTH = G_HEADS * G_HEAD_DIM
G_CHUNK = 64
MIX_WIDTH = A_WIDTH + R_WIDTH + G_WIDTH
DEEPNORM_ALPHA = (2 * DEPTH) ** 0.25
LN_EPS = 1e-5
RMS_EPS = 1e-6

LANES = 128
SUBLANES = 8
VMEM_LIMIT = 56 * 1024 * 1024

COL_RX = 0
COL_RZ = 1024
COL_AQ = 2048
COL_AZ = 2560
COL_GQ = 3072
COL_GK = 3584
COL_GV = 4096
COL_GZ = 4608
COL_KV = 5120
COL_SMALL = 5376
N_PROJ = 5632

F32 = jnp.float32
BF16 = jnp.bfloat16
HIGHEST = lax.Precision.HIGHEST


def _silu(x):
    return x * jax.nn.sigmoid(x)


def _softplus(x):
    return jnp.maximum(x, 0.0) + jnp.log1p(jnp.exp(-jnp.abs(x)))


N_IN = 5384
IN_TM = 2048
IN_TN = 512
IN_SUB = 256
IN_TILE_ORDER = (5, 6, 7, 8, 9, 10, 11, 12, 0, 1, 3, 4, 13, 14, 15, 16, 17, 18, 19, 20, 2, 21)


def _inproj_kernel(order_ref, x_ref, w0_ref, w1_ref, o_ref, wb):
    j = pl.program_id(0)

    @pl.when(pl.program_id(1) == 0)
    def _():
        col = lax.broadcasted_iota(jnp.int32, (D_MODEL, IN_SUB), 1)
        for t, w_ref in enumerate((w0_ref, w1_ref)):
            valid = N_IN - order_ref[2 * j + t] * IN_SUB
            wb[:, t * IN_SUB:(t + 1) * IN_SUB] = jnp.where(col < valid, w_ref[...], 0.0).astype(BF16)

    o_ref[...] = jnp.dot(x_ref[...], wb[...], preferred_element_type=F32)


def _inproj(xb, w_in, layer):
    T = xb.shape[0]
    per_step = IN_TN // IN_SUB
    assert per_step == 2 and len(IN_TILE_ORDER) * IN_SUB == N_PROJ
    order = jnp.asarray(IN_TILE_ORDER, jnp.int32)
    w_spec = lambda t: pl.BlockSpec((None, D_MODEL, IN_SUB),
                                    lambda j, i, order_ref: (layer, 0, order_ref[per_step * j + t]))
    return pl.pallas_call(
        _inproj_kernel,
        grid_spec=pltpu.PrefetchScalarGridSpec(
            num_scalar_prefetch=1,
            grid=(N_PROJ // IN_TN, T // IN_TM),
            in_specs=[pl.BlockSpec((IN_TM, D_MODEL), lambda j, i, order_ref: (i, 0)),
                      w_spec(0), w_spec(1)],
            out_specs=pl.BlockSpec((IN_TM, IN_TN), lambda j, i, order_ref: (i, j)),
            scratch_shapes=[pltpu.VMEM((D_MODEL, IN_TN), BF16)]),
        out_shape=jax.ShapeDtypeStruct((T, N_PROJ), F32),
        compiler_params=pltpu.CompilerParams(
            dimension_semantics=("arbitrary", "arbitrary"),
            vmem_limit_bytes=VMEM_LIMIT),
        name="inproj",
    )(order, xb, w_in, w_in)


def _rope(x, cos, sin_signed):
    w = x.shape[1]
    reps = w // LANES
    if reps > 1:
        cos = jnp.concatenate([cos] * reps, axis=1)
        sin_signed = jnp.concatenate([sin_signed] * reps, axis=1)
    lane = lax.broadcasted_iota(jnp.int32, x.shape, 1)
    first_half = (lane % A_HEAD_DIM) < (A_HEAD_DIM // 2)
    swapped = jnp.where(first_half,
                        pltpu.roll(x, w - A_HEAD_DIM // 2, 1),
                        pltpu.roll(x, A_HEAD_DIM // 2, 1))
    return x * cos + swapped * sin_signed


def _attn_kernel(sinks_ref, q_ref, kvc_ref, kvp_ref, az_ref, cc_ref, sc_ref, cp_ref, sp_ref, o_ref):
    blk = pl.program_id(1)
    grp = A_HEADS // A_KV_HEADS
    q = _rope(q_ref[...], cc_ref[...], sc_ref[...]) * (A_HEAD_DIM ** -0.5)
    kvc = kvc_ref[...]
    kvp = kvp_ref[...]
    k_cur = _rope(kvc[:, :A_KV_WIDTH], cc_ref[...], sc_ref[...])
    k_prev = _rope(kvp[:, :A_KV_WIDTH], cp_ref[...], sp_ref[...])
    k_all = jnp.concatenate([k_prev, k_cur], axis=0).astype(BF16)
    v_all = jnp.concatenate([kvp[:, A_KV_WIDTH:], kvc[:, A_KV_WIDTH:]], axis=0).astype(BF16)
    qb = q.astype(BF16)

    rows = grp * A_BLOCK
    i = lax.broadcasted_iota(jnp.int32, (rows, 2 * A_BLOCK), 0) % A_BLOCK
    j = lax.broadcasted_iota(jnp.int32, (rows, 2 * A_BLOCK), 1)
    diff = i - j + A_BLOCK
    mask = (diff >= 0) & (diff < WINDOW) & ((j >= A_BLOCK) | (blk > 0))
    slab = lax.broadcasted_iota(jnp.int32, (rows, 1), 0) // A_BLOCK

    outs = []
    for g in range(A_KV_HEADS):
        kg = k_all[:, g * A_HEAD_DIM:(g + 1) * A_HEAD_DIM]
        vg = v_all[:, g * A_HEAD_DIM:(g + 1) * A_HEAD_DIM]
        qg = jnp.concatenate(
            [qb[:, (g * grp + h) * A_HEAD_DIM:(g * grp + h + 1) * A_HEAD_DIM] for h in range(grp)],
            axis=0)
        s = lax.dot_general(qg, kg, (((1,), (1,)), ((), ())), preferred_element_type=F32)
        s = jnp.where(mask, s, -jnp.inf)
        sink = jnp.zeros((rows, 1), F32)
        for h in range(grp):
            sink = jnp.where(slab == h, sinks_ref[g * grp + h], sink)
        m = jnp.maximum(jnp.max(s, axis=-1, keepdims=True), sink)
        p = jnp.exp(s - m)
        denom = jnp.sum(p, axis=-1, keepdims=True) + jnp.exp(sink - m)
        og = jnp.dot(p.astype(BF16), vg, preferred_element_type=F32) / denom
        for h in range(grp):
            outs.append(og[h * A_BLOCK:(h + 1) * A_BLOCK, :])
    out = jnp.concatenate(outs, axis=1)
    o_ref[...] = out * _silu(az_ref[...])


def _attention(proj, sinks, cos_t, sin_t, batch, seq):
    nb = seq // A_BLOCK
    T = batch * seq
    row = lambda b, i: b * nb + i
    prow = lambda b, i: b * nb + jnp.maximum(i - 1, 0)
    return pl.pallas_call(
        _attn_kernel,
        grid=(batch, nb),
        in_specs=[
            pl.BlockSpec(memory_space=pltpu.SMEM),
            pl.BlockSpec((A_BLOCK, A_WIDTH), lambda b, i: (row(b, i), COL_AQ // A_WIDTH)),
            pl.BlockSpec((A_BLOCK, 2 * A_KV_WIDTH), lambda b, i: (row(b, i), COL_KV // (2 * A_KV_WIDTH))),
            pl.BlockSpec((A_BLOCK, 2 * A_KV_WIDTH), lambda b, i: (prow(b, i), COL_KV // (2 * A_KV_WIDTH))),
            pl.BlockSpec((A_BLOCK, A_WIDTH), lambda b, i: (row(b, i), COL_AZ // A_WIDTH)),
            pl.BlockSpec((A_BLOCK, LANES), lambda b, i: (i, 0)),
            pl.BlockSpec((A_BLOCK, LANES), lambda b, i: (i, 0)),
            pl.BlockSpec((A_BLOCK, LANES), lambda b, i: (jnp.maximum(i - 1, 0), 0)),
            pl.BlockSpec((A_BLOCK, LANES), lambda b, i: (jnp.maximum(i - 1, 0), 0)),
        ],
        out_specs=pl.BlockSpec((A_BLOCK, A_WIDTH), lambda b, i: (row(b, i), 0)),
        out_shape=jax.ShapeDtypeStruct((T, A_WIDTH), F32),
        compiler_params=pltpu.CompilerParams(
            dimension_semantics=("parallel", "arbitrary"),
            vmem_limit_bytes=VMEM_LIMIT),
        name="swa_attention",
    )(sinks, proj, proj, proj, proj, cos_t, sin_t, cos_t, sin_t)


R_TB = 256


def _causal_conv(buf_ref, w, tb):
    acc = None
    for k in range(CONV_WIDTH):
        off = SUBLANES - (CONV_WIDTH - 1) + k
        term = buf_ref[off:off + tb, :] * w[k:k + 1, :]
        acc = term if acc is None else acc + term
    return acc


def _rglru_kernel(rx_ref, rz_ref, cw_ref, cb_ref, wa_ref, ba_ref, wx_ref, bx_ref, lam_ref,
                  o_ref, xbuf, acum, ucum, hcarry):
    tb = R_TB

    @pl.when(pl.program_id(1) == 0)
    def _():
        xbuf[0:SUBLANES, :] = jnp.zeros((SUBLANES, R_WIDTH), F32)
        hcarry[...] = jnp.zeros_like(hcarry)

    xbuf[SUBLANES:SUBLANES + tb, :] = rx_ref[...]
    xr = _causal_conv(xbuf, cw_ref[...], tb) + cb_ref[...]
    xbuf[0:SUBLANES, :] = xbuf[tb:tb + SUBLANES, :]

    xrb = xr.astype(BF16)
    ra, ia = [], []
    for n in range(R_BLOCKS):
        xb = xrb[:, n * R_BLOCK_DIM:(n + 1) * R_BLOCK_DIM]
        ra.append(jnp.dot(xb, wa_ref[n], preferred_element_type=F32))
        ia.append(jnp.dot(xb, wx_ref[n], preferred_element_type=F32))
    r = jax.nn.sigmoid(jnp.concatenate(ra, axis=1) + ba_ref[...])
    ig = jax.nn.sigmoid(jnp.concatenate(ia, axis=1) + bx_ref[...])
    log_a = (-R_C) * r * _softplus(-lam_ref[...])
    a = jnp.exp(log_a)
    u = jnp.sqrt(jnp.maximum(1.0 - a * a, 0.0)) * (ig * xr)

    row8 = lax.broadcasted_iota(jnp.int32, (tb, R_WIDTH), 0) % SUBLANES
    for k in (1, 2, 4):
        keep = row8 >= k
        a_sh = jnp.where(keep, pltpu.roll(a, k, 0), 1.0)
        u_sh = jnp.where(keep, pltpu.roll(u, k, 0), 0.0)
        u = a * u_sh + u
        a = a * a_sh
    acum[...] = a
    ucum[...] = u

    def body(g, hprev):
        r0 = pl.multiple_of(g * SUBLANES, SUBLANES)
        h = acum[pl.ds(r0, SUBLANES), :] * hprev + ucum[pl.ds(r0, SUBLANES), :]
        ucum[pl.ds(r0, SUBLANES), :] = h
        return jnp.broadcast_to(h[SUBLANES - 1:SUBLANES, :], (SUBLANES, R_WIDTH))

    hlast = lax.fori_loop(0, tb // SUBLANES, body, hcarry[...])
    hcarry[...] = hlast
    o_ref[...] = ucum[...] * _silu(rz_ref[...])


def _rglru(proj, cw, cb, wa, ba, wx, bx, lam, batch, seq):
    tb = R_TB
    nblk = seq // tb
    T = batch * seq
    vec = lambda: pl.BlockSpec((1, R_WIDTH), lambda b, i: (0, 0))
    return pl.pallas_call(
        _rglru_kernel,
        grid=(batch, nblk),
        in_specs=[
            pl.BlockSpec((tb, R_WIDTH), lambda b, i: (b * nblk + i, COL_RX // R_WIDTH)),
            pl.BlockSpec((tb, R_WIDTH), lambda b, i: (b * nblk + i, COL_RZ // R_WIDTH)),
            pl.BlockSpec((CONV_WIDTH, R_WIDTH), lambda b, i: (0, 0)),
            vec(),
            pl.BlockSpec((R_BLOCKS, R_BLOCK_DIM, R_BLOCK_DIM), lambda b, i: (0, 0, 0)),
            vec(),
            pl.BlockSpec((R_BLOCKS, R_BLOCK_DIM, R_BLOCK_DIM), lambda b, i: (0, 0, 0)),
            vec(),
            vec(),
        ],
        out_specs=pl.BlockSpec((tb, R_WIDTH), lambda b, i: (b * nblk + i, 0)),
        out_shape=jax.ShapeDtypeStruct((T, R_WIDTH), F32),
        scratch_shapes=[
            pltpu.VMEM((SUBLANES + tb, R_WIDTH), F32),
            pltpu.VMEM((tb, R_WIDTH), F32),
            pltpu.VMEM((tb, R_WIDTH), F32),
            pltpu.VMEM((SUBLANES, R_WIDTH), F32),
        ],
        compiler_params=pltpu.CompilerParams(
            dimension_semantics=("parallel", "arbitrary"),
            vmem_limit_bytes=VMEM_LIMIT),
        name="rglru",
    )(proj, proj, cw, cb, wa, ba, wx, bx, lam)


G_TB = 256


def _l2norm(x):
    return x * lax.rsqrt(jnp.sum(x * x, axis=-1, keepdims=True) + RMS_EPS)


def _bdot(a, b):
    return jnp.dot(a.astype(BF16), b.astype(BF16), preferred_element_type=F32)


def _unit_lower_solves(ms, rhss):
    c = ms[0].shape[0]
    eye = (lax.broadcasted_iota(jnp.int32, (c, c), 0) ==
           lax.broadcasted_iota(jnp.int32, (c, c), 1)).astype(F32)
    p_s = [eye - m for m in ms]
    a_s = [_bdot(m, m) for m in ms]
    k = 2
    while k < c // 2:
        st = [_bdot(jnp.concatenate([a, p], axis=0), a) for a, p in zip(a_s, p_s)]
        p_s = [p + s[c:] for p, s in zip(p_s, st)]
        a_s = [s[:c] for s in st]
        k *= 2
    ys = [r + _bdot(a, r) for a, r in zip(a_s, rhss)]
    return [_bdot(p, y) for p, y in zip(p_s, ys)]


G_GROUP = 2


def _gdn_kernel(gq_ref, gk_ref, gv_ref, gz_ref, sm_ref, cwq_ref, cwk_ref, cwv_ref,
                alog_ref, dtb_ref, nw_ref, o_ref,
                qbuf, kbuf, vbuf, qs, ks, vs, bg, us, ws, qds, qks, kdts, gls, state):
    tb = G_TB
    C = G_CHUNK
    nb = gq_ref.shape[0]

    @pl.when(pl.program_id(0) == 0)
    def _():
        zeros = jnp.zeros((nb, SUBLANES, G_WIDTH), F32)
        qbuf[:, 0:SUBLANES, :] = zeros
        kbuf[:, 0:SUBLANES, :] = zeros
        vbuf[:, 0:SUBLANES, :] = zeros
        state[...] = jnp.zeros_like(state)

    lane = lax.broadcasted_iota(jnp.int32, (tb, LANES), 1)
    for b in range(nb):
        for src, buf, cw, dst in ((gq_ref, qbuf, cwq_ref, qs), (gk_ref, kbuf, cwk_ref, ks),
                                  (gv_ref, vbuf, cwv_ref, vs)):
            buf[b, SUBLANES:SUBLANES + tb, :] = src[b]
            dst[b] = _silu(_causal_conv(buf.at[b], cw[...], tb))
            buf[b, 0:SUBLANES, :] = buf[b, tb:tb + SUBLANES, :]
        small = sm_ref[b]
        g_all = -jnp.exp(alog_ref[...]) * _softplus(small + dtb_ref[...])
        bg[b] = jnp.where(lane < G_HEADS, jax.nn.sigmoid(small), g_all)

    ri = lax.broadcasted_iota(jnp.int32, (C, C), 0)
    ci = lax.broadcasted_iota(jnp.int32, (C, C), 1)
    tril = ri >= ci
    strict = ri > ci
    tril_f = tril.astype(F32)

    def intra(grp, carry):
        dests, ms, rhss = [], [], []
        for b in range(nb):
            for cc in range(G_GROUP):
                c = grp * G_GROUP + cc
                r0 = pl.multiple_of(c * C, C)
                bgc = bg[b, pl.ds(r0, C), :]
                gcum_col = jnp.dot(tril_f, bgc, precision=HIGHEST, preferred_element_type=F32)
                gcum_row = gcum_col.T[:SUBLANES, :]
                for h in range(G_HEADS):
                    cols = slice(h * G_HEAD_DIM, (h + 1) * G_HEAD_DIM)
                    qn = _l2norm(qs[b, pl.ds(r0, C), cols]) * (G_HEAD_DIM ** -0.5)
                    kn = _l2norm(ks[b, pl.ds(r0, C), cols])
                    v = vs[b, pl.ds(r0, C), cols]
                    beta = jnp.broadcast_to(bgc[:, h:h + 1], (C, G_HEAD_DIM))
                    gc = jnp.broadcast_to(gcum_col[:, G_HEADS + h:G_HEADS + h + 1], (C, G_HEAD_DIM))
                    gr = gcum_row[G_HEADS + h:G_HEADS + h + 1, :]
                    g_last = gc[C - 1:C, :]
                    decay = jnp.where(tril, jnp.exp(gc[:, :C] - gr), 0.0)
                    eg = jnp.exp(gc)
                    knb = kn.astype(BF16)
                    qk_kk = lax.dot_general(jnp.concatenate([qn.astype(BF16), knb], axis=0), knb,
                                            (((1,), (1,)), ((), ())), preferred_element_type=F32)
                    qks[b, pl.ds(r0, C), h * C:(h + 1) * C] = (qk_kk[:C] * decay).astype(BF16)
                    ms.append(jnp.where(strict, qk_kk[C:] * beta[:, :C] * decay, 0.0))
                    qds[b, pl.ds(r0, C), cols] = (qn * eg).astype(BF16)
                    kdts[b, c * G_HEADS + h] = (kn * jnp.exp(g_last - gc)).T.astype(BF16)
                    gls[b, c * G_HEADS + h] = jnp.broadcast_to(jnp.exp(g_last), (SUBLANES, LANES))
                    rhss.append(jnp.concatenate([v * beta, kn * (beta * eg)], axis=1))
                    dests.append((b, r0, cols))
        for (b, r0, cols), uw in zip(dests, _unit_lower_solves(ms, rhss)):
            us[b, pl.ds(r0, C), cols] = uw[:, :G_HEAD_DIM]
            ws[b, pl.ds(r0, C), cols] = uw[:, G_HEAD_DIM:].astype(BF16)
        return carry

    lax.fori_loop(0, tb // (C * G_GROUP), intra, 0)

    chains = [(b, h) for b in range(nb) for h in range(G_HEADS)]

    def inter(c, carry):
        r0 = pl.multiple_of(c * C, C)
        rows = pl.ds(r0, C)
        hcols = lambda h: slice(h * G_HEAD_DIM, (h + 1) * G_HEAD_DIM)
        s_old = [state[b, h] for b, h in chains]
        sb = [s.astype(BF16) for s in s_old]
        w_s = [jnp.dot(ws[b, rows, hcols(h)], s, preferred_element_type=F32)
               for (b, h), s in zip(chains, sb)]
        vnb = [(us[b, rows, hcols(h)] - x).astype(BF16) for (b, h), x in zip(chains, w_s)]
        upd = [jnp.dot(kdts[b, c * G_HEADS + h], v, preferred_element_type=F32)
               for (b, h), v in zip(chains, vnb)]
        for (b, h), s, x in zip(chains, s_old, upd):
            state[b, h] = s * gls[b, c * G_HEADS + h][0:1, :] + x
        for (b, h), s, v in zip(chains, sb, vnb):
            o = (jnp.dot(qds[b, rows, hcols(h)], s, preferred_element_type=F32)
                 + jnp.dot(qks[b, rows, h * C:(h + 1) * C], v, preferred_element_type=F32))
            on = o * lax.rsqrt(jnp.mean(o * o, axis=-1, keepdims=True) + RMS_EPS) * nw_ref[...]
            o_ref[b, rows, hcols(h)] = on * _silu(gz_ref[b, rows, hcols(h)])
        return carry

    lax.fori_loop(0, tb // C, inter, 0)


def _gdn(proj, cwq, cwk, cwv, alog, dtb, nw, batch, seq):
    tb = G_TB
    nchunk = tb // G_CHUNK
    proj3 = proj.reshape(batch, seq, N_PROJ)
    seg = lambda col: pl.BlockSpec((batch, tb, G_WIDTH), lambda i: (0, i, col // G_WIDTH))
    cw = lambda: pl.BlockSpec((CONV_WIDTH, G_WIDTH), lambda i: (0, 0))
    vec = lambda: pl.BlockSpec((1, LANES), lambda i: (0, 0))
    out = pl.pallas_call(
        _gdn_kernel,
        grid=(seq // tb,),
        in_specs=[seg(COL_GQ), seg(COL_GK), seg(COL_GV), seg(COL_GZ),
                  pl.BlockSpec((batch, tb, LANES), lambda i: (0, i, COL_SMALL // LANES)),
                  cw(), cw(), cw(), vec(), vec(), vec()],
        out_specs=pl.BlockSpec((batch, tb, G_WIDTH), lambda i: (0, i, 0)),
        out_shape=jax.ShapeDtypeStruct((batch, seq, G_WIDTH), F32),
        scratch_shapes=[
            pltpu.VMEM((batch, SUBLANES + tb, G_WIDTH), F32),
            pltpu.VMEM((batch, SUBLANES + tb, G_WIDTH), F32),
            pltpu.VMEM((batch, SUBLANES + tb, G_WIDTH), F32),
            pltpu.VMEM((batch, tb, G_WIDTH), F32),
            pltpu.VMEM((batch, tb, G_WIDTH), F32),
            pltpu.VMEM((batch, tb, G_WIDTH), F32),
            pltpu.VMEM((batch, tb, LANES), F32),
            pltpu.VMEM((batch, tb, G_WIDTH), F32),
            pltpu.VMEM((batch, tb, G_WIDTH), BF16),
            pltpu.VMEM((batch, tb, G_WIDTH), BF16),
            pltpu.VMEM((batch, tb, G_HEADS * G_CHUNK), BF16),
            pltpu.VMEM((batch, nchunk * G_HEADS, G_HEAD_DIM, G_CHUNK), BF16),
            pltpu.VMEM((batch, nchunk * G_HEADS, SUBLANES, LANES), F32),
            pltpu.VMEM((batch, G_HEADS, G_HEAD_DIM, G_HEAD_DIM), F32),
        ],
        compiler_params=pltpu.CompilerParams(
            dimension_semantics=("arbitrary",),
            vmem_limit_bytes=VMEM_LIMIT),
        name="gated_deltanet",
    )(proj3, proj3, proj3, proj3, proj3, cwq, cwk, cwv, alog, dtb, nw)
    return out.reshape(batch * seq, G_WIDTH)


def _outproj_kernel(ya_ref, yr_ref, yg_ref, x_ref, w_ref, g_ref, b_ref, o_ref, ob_ref):
    y = jnp.dot(ya_ref[...].astype(BF16), w_ref[0:A_WIDTH, :], preferred_element_type=F32)
    y += jnp.dot(yr_ref[...].astype(BF16), w_ref[A_WIDTH:A_WIDTH + R_WIDTH, :],
                 preferred_element_type=F32)
    y += jnp.dot(yg_ref[...].astype(BF16), w_ref[A_WIDTH + R_WIDTH:MIX_WIDTH, :],
                 preferred_element_type=F32)
    z = DEEPNORM_ALPHA * x_ref[...] + y
    mu = jnp.mean(z, axis=-1, keepdims=True)
    zc = z - mu
    var = jnp.mean(zc * zc, axis=-1, keepdims=True)
    out = zc * lax.rsqrt(var + LN_EPS) * g_ref[...] + b_ref[...]
    o_ref[...] = out
    ob_ref[...] = out.astype(BF16)


def _outproj(ya, yr, yg, x, w, layer, g, b):
    T = x.shape[0]
    tm = 256
    row = lambda width: pl.BlockSpec((tm, width), lambda i: (i, 0))
    vec = lambda: pl.BlockSpec((1, D_MODEL), lambda i: (0, 0))
    return pl.pallas_call(
        _outproj_kernel,
        grid=(T // tm,),
        in_specs=[row(A_WIDTH), row(R_WIDTH), row(G_WIDTH), row(D_MODEL),
                  pl.BlockSpec((None, MIX_WIDTH, D_MODEL), lambda i: (layer, 0, 0)), vec(), vec()],
        out_specs=[row(D_MODEL), row(D_MODEL)],
        out_shape=[jax.ShapeDtypeStruct((T, D_MODEL), F32),
                   jax.ShapeDtypeStruct((T, D_MODEL), BF16)],
        compiler_params=pltpu.CompilerParams(
            dimension_semantics=("parallel",),
            vmem_limit_bytes=VMEM_LIMIT),
        name="outproj_deepnorm",
    )(ya, yr, yg, x, w, g, b)


def _rope_tables(seq):
    half = A_HEAD_DIM // 2
    inv = 1.0 / (ROPE_THETA ** (jnp.arange(0, A_HEAD_DIM, 2, dtype=F32) / A_HEAD_DIM))
    ang = jnp.arange(seq, dtype=F32)[:, None] * inv[None, :]
    cos, sin = jnp.cos(ang), jnp.sin(ang)
    reps = LANES // A_HEAD_DIM
    cos_t = jnp.tile(jnp.concatenate([cos, cos], axis=1), (1, reps))
    sin_t = jnp.tile(jnp.concatenate([-sin, sin], axis=1), (1, reps))
    assert cos_t.shape == (seq, LANES) and half * 2 == A_HEAD_DIM
    return cos_t, sin_t


def _lane_vec(v, offset):
    return jnp.zeros((1, LANES), F32).at[0, offset:offset + v.shape[0]].set(v.astype(F32))


@jax.jit
def _forward(x, w_in, sinks, r_conv_w, r_conv_b, r_wa, r_ba, r_wx, r_bx, r_lam,
             g_conv_w, g_a_log, g_dt_bias, g_norm_w, w_out, ln_g, ln_b):
    batch, seq, _ = x.shape
    T = batch * seq
    cos_t, sin_t = _rope_tables(seq)
    w_out_b = w_out.astype(BF16)
    xf = x.reshape(T, D_MODEL)
    xb = xf.astype(BF16)
    for l in range(DEPTH):
        proj = _inproj(xb, w_in, l)
        ya = _attention(proj, sinks[l], cos_t, sin_t, batch, seq)
        yr = _rglru(proj, r_conv_w[l], r_conv_b[l][None, :], r_wa[l].astype(BF16), r_ba[l][None, :],
                    r_wx[l].astype(BF16), r_bx[l][None, :], r_lam[l][None, :], batch, seq)
        gcw = g_conv_w[l]
        yg = _gdn(proj, gcw[:, :G_WIDTH], gcw[:, G_WIDTH:2 * G_WIDTH], gcw[:, 2 * G_WIDTH:],
                  _lane_vec(g_a_log[l], G_HEADS), _lane_vec(g_dt_bias[l], G_HEADS),
                  g_norm_w[l][None, :], batch, seq)
        xf, xb = _outproj(ya, yr, yg, xf, w_out_b, l, ln_g[l][None, :], ln_b[l][None, :])
    return xf.reshape(batch, seq, D_MODEL)


def kernel(x, w_in, sinks, r_conv_w, r_conv_b, r_wa, r_ba, r_wx, r_bx, r_lam, g_conv_w, g_a_log,
           g_dt_bias, g_norm_w, w_out, ln_g, ln_b):
    return _forward(x, w_in, sinks, r_conv_w, r_conv_b, r_wa, r_ba, r_wx, r_bx, r_lam,
                    g_conv_w, g_a_log, g_dt_bias, g_norm_w, w_out, ln_g, ln_b)
```

```python
import functools
import math

import numpy as np
import jax
import jax.numpy as jnp
from jax import lax
from jax.experimental import pallas as pl
from jax.experimental.pallas import tpu as pltpu

D_MODEL = 2048
DEPTH = 2
A_HEADS = 8
A_KV_HEADS = 2
A_HEAD_DIM = 64
A_WIDTH = A_HEADS * A_HEAD_DIM
A_KV_WIDTH = A_KV_HEADS * A_HEAD_DIM
WINDOW = 128
A_BLOCK = 128
ROPE_THETA = 10000.0
R_WIDTH = 1024
R_BLOCKS = 8
R_BLOCK_DIM = R_WIDTH // R_BLOCKS
R_C = 8.0
CONV_WIDTH = 4
G_HEADS = 4
G_HEAD_DIM = 128
G_WIDTH = G_HEADS * G_HEAD_DIM
G_CHUNK = 64
MIX_WIDTH = A_WIDTH + R_WIDTH + G_WIDTH
DEEPNORM_ALPHA = (2 * DEPTH) ** 0.25
LN_EPS = 1e-5
RMS_EPS = 1e-6

LANES = 128
SUBLANES = 8
VMEM_LIMIT = 56 * 1024 * 1024

COL_RX = 0
COL_RZ = 1024
COL_AQ = 2048
COL_AZ = 2560
COL_GQ = 3072
COL_GK = 3584
COL_GV = 4096
COL_GZ = 4608
COL_KV = 5120
COL_SMALL = 5376
N_PROJ = 5632

F32 = jnp.float32
BF16 = jnp.bfloat16
HIGHEST = lax.Precision.HIGHEST


def _sigmoid(x):
    return 0.5 * jnp.tanh(0.5 * x) + 0.5


def _silu(x):
    return x * _sigmoid(x)


def _softplus(x):
    return jnp.maximum(x, 0.0) + jnp.log1p(jnp.exp(-jnp.abs(x)))


N_IN = 5384
IN_TM = 2048
IN_TN = 512
IN_SUB = 256
IN_TILE_ORDER = (5, 6, 7, 8, 9, 10, 11, 12, 0, 1, 3, 4, 13, 14, 15, 16, 17, 18, 19, 20, 2, 21)


def _inproj_kernel(order_ref, x_ref, w0_ref, w1_ref, o_ref, wb):
    j = pl.program_id(0)

    @pl.when(pl.program_id(1) == 0)
    def _():
        row = lax.broadcasted_iota(jnp.int32, (IN_SUB, D_MODEL), 0)
        for t, w_ref in enumerate((w0_ref, w1_ref)):
            valid = N_IN - order_ref[2 * j + t] * IN_SUB
            wb[t * IN_SUB:(t + 1) * IN_SUB, :] = jnp.where(row < valid, w_ref[...], 0.0).astype(BF16)

    o_ref[...] = lax.dot_general(x_ref[...], wb[...], (((1,), (1,)), ((), ())),
                                 preferred_element_type=F32)


def _inproj(xb, w_in_t, layer):
    T = xb.shape[0]
    per_step = IN_TN // IN_SUB
    assert per_step == 2 and len(IN_TILE_ORDER) * IN_SUB == N_PROJ
    order = jnp.asarray(IN_TILE_ORDER, jnp.int32)
    w_spec = lambda t: pl.BlockSpec((None, IN_SUB, D_MODEL),
                                    lambda j, i, order_ref: (layer, order_ref[per_step * j + t], 0))
    return pl.pallas_call(
        _inproj_kernel,
        grid_spec=pltpu.PrefetchScalarGridSpec(
            num_scalar_prefetch=1,
            grid=(N_PROJ // IN_TN, T // IN_TM),
            in_specs=[pl.BlockSpec((IN_TM, D_MODEL), lambda j, i, order_ref: (i, 0)),
                      w_spec(0), w_spec(1)],
            out_specs=pl.BlockSpec((IN_TM, IN_TN), lambda j, i, order_ref: (i, j)),
            scratch_shapes=[pltpu.VMEM((IN_TN, D_MODEL), BF16)]),
        out_shape=jax.ShapeDtypeStruct((T, N_PROJ), F32),
        compiler_params=pltpu.CompilerParams(
            dimension_semantics=("arbitrary", "arbitrary"),
            vmem_limit_bytes=VMEM_LIMIT),
        name="inproj",
    )(order, xb, w_in_t, w_in_t)


def _rope(x, cos, sin_signed):
    w = x.shape[1]
    reps = w // LANES
    if reps > 1:
        cos = jnp.concatenate([cos] * reps, axis=1)
        sin_signed = jnp.concatenate([sin_signed] * reps, axis=1)
    lane = lax.broadcasted_iota(jnp.int32, x.shape, 1)
    first_half = (lane % A_HEAD_DIM) < (A_HEAD_DIM // 2)
    swapped = jnp.where(first_half,
                        pltpu.roll(x, w - A_HEAD_DIM // 2, 1),
                        pltpu.roll(x, A_HEAD_DIM // 2, 1))
    return x * cos + swapped * sin_signed


def _attn_kernel(sinks_ref, q_ref, kvc_ref, kvp_ref, az_ref, cc_ref, sc_ref, cp_ref, sp_ref, o_ref):
    blk = pl.program_id(1)
    grp = A_HEADS // A_KV_HEADS
    q = _rope(q_ref[...], cc_ref[...], sc_ref[...]) * (A_HEAD_DIM ** -0.5)
    kvc = kvc_ref[...]
    kvp = kvp_ref[...]
    k_cur = _rope(kvc[:, :A_KV_WIDTH], cc_ref[...], sc_ref[...])
    k_prev = _rope(kvp[:, :A_KV_WIDTH], cp_ref[...], sp_ref[...])
    k_all = jnp.concatenate([k_prev, k_cur], axis=0).astype(BF16)
    v_all = jnp.concatenate([kvp[:, A_KV_WIDTH:], kvc[:, A_KV_WIDTH:]], axis=0).astype(BF16)
    qb = q.astype(BF16)

    rows = grp * A_BLOCK
    i = lax.broadcasted_iota(jnp.int32, (rows, 2 * A_BLOCK), 0) % A_BLOCK
    j = lax.broadcasted_iota(jnp.int32, (rows, 2 * A_BLOCK), 1)
    diff = i - j + A_BLOCK
    mask = (diff >= 0) & (diff < WINDOW) & ((j >= A_BLOCK) | (blk > 0))
    slab = lax.broadcasted_iota(jnp.int32, (rows, 1), 0) // A_BLOCK

    outs = []
    for g in range(A_KV_HEADS):
        kg = k_all[:, g * A_HEAD_DIM:(g + 1) * A_HEAD_DIM]
        vg = v_all[:, g * A_HEAD_DIM:(g + 1) * A_HEAD_DIM]
        qg = jnp.concatenate(
            [qb[:, (g * grp + h) * A_HEAD_DIM:(g * grp + h + 1) * A_HEAD_DIM] for h in range(grp)],
            axis=0)
        s = lax.dot_general(qg, kg, (((1,), (1,)), ((), ())), preferred_element_type=F32)
        s = jnp.where(mask, s, -jnp.inf)
        sink = jnp.zeros((rows, 1), F32)
        for h in range(grp):
            sink = jnp.where(slab == h, sinks_ref[g * grp + h], sink)
        m = jnp.maximum(jnp.max(s, axis=-1, keepdims=True), sink)
        p = jnp.exp(s - m)
        denom = jnp.sum(p, axis=-1, keepdims=True) + jnp.exp(sink - m)
        og = jnp.dot(p.astype(BF16), vg, preferred_element_type=F32) / denom
        for h in range(grp):
            outs.append(og[h * A_BLOCK:(h + 1) * A_BLOCK, :])
    out = jnp.concatenate(outs, axis=1)
    o_ref[...] = out * _silu(az_ref[...])


def _attention(proj, sinks, cos_t, sin_t, batch, seq):
    nb = seq // A_BLOCK
    T = batch * seq
    row = lambda b, i: b * nb + i
    prow = lambda b, i: b * nb + jnp.maximum(i - 1, 0)
    return pl.pallas_call(
        _attn_kernel,
        grid=(batch, nb),
        in_specs=[
            pl.BlockSpec(memory_space=pltpu.SMEM),
            pl.BlockSpec((A_BLOCK, A_WIDTH), lambda b, i: (row(b, i), COL_AQ // A_WIDTH)),
            pl.BlockSpec((A_BLOCK, 2 * A_KV_WIDTH), lambda b, i: (row(b, i), COL_KV // (2 * A_KV_WIDTH))),
            pl.BlockSpec((A_BLOCK, 2 * A_KV_WIDTH), lambda b, i: (prow(b, i), COL_KV // (2 * A_KV_WIDTH))),
            pl.BlockSpec((A_BLOCK, A_WIDTH), lambda b, i: (row(b, i), COL_AZ // A_WIDTH)),
            pl.BlockSpec((A_BLOCK, LANES), lambda b, i: (i, 0)),
            pl.BlockSpec((A_BLOCK, LANES), lambda b, i: (i, 0)),
            pl.BlockSpec((A_BLOCK, LANES), lambda b, i: (jnp.maximum(i - 1, 0), 0)),
            pl.BlockSpec((A_BLOCK, LANES), lambda b, i: (jnp.maximum(i - 1, 0), 0)),
        ],
        out_specs=pl.BlockSpec((A_BLOCK, A_WIDTH), lambda b, i: (row(b, i), 0)),
        out_shape=jax.ShapeDtypeStruct((T, A_WIDTH), F32),
        compiler_params=pltpu.CompilerParams(
            dimension_semantics=("parallel", "arbitrary"),
            vmem_limit_bytes=VMEM_LIMIT),
        name="swa_attention",
    )(sinks, proj, proj, proj, proj, cos_t, sin_t, cos_t, sin_t)


R_TB = 256
SQRT_FLOOR = 1e-30


def _causal_conv(buf_ref, w, tb):
    xext = buf_ref[0:SUBLANES + tb, :]
    acc = xext[SUBLANES:, :] * w[CONV_WIDTH - 1:CONV_WIDTH, :]
    for k in range(CONV_WIDTH - 1):
        shifted = pltpu.roll(xext, CONV_WIDTH - 1 - k, 0)[SUBLANES:, :]
        acc = acc + shifted * w[k:k + 1, :]
    return acc


def _rglru_kernel(rx_ref, rz_ref, cw_ref, cb_ref, wa_ref, ba_ref, wx_ref, bx_ref, lam_ref,
                  o_ref, xbuf, acum, ucum, hcarry):
    tb = R_TB

    @pl.when(pl.program_id(1) == 0)
    def _():
        xbuf[0:SUBLANES, :] = jnp.zeros((SUBLANES, R_WIDTH), F32)
        hcarry[...] = jnp.zeros_like(hcarry)

    xbuf[SUBLANES:SUBLANES + tb, :] = rx_ref[...]
    xr = _causal_conv(xbuf, cw_ref[...], tb) + cb_ref[...]
    xbuf[0:SUBLANES, :] = xbuf[tb:tb + SUBLANES, :]

    xrb = xr.astype(BF16)
    ra, ia = [], []
    for n in range(R_BLOCKS):
        xb = xrb[:, n * R_BLOCK_DIM:(n + 1) * R_BLOCK_DIM]
        ra.append(jnp.dot(xb, wa_ref[n], preferred_element_type=F32))
        ia.append(jnp.dot(xb, wx_ref[n], preferred_element_type=F32))
    r = _sigmoid(jnp.concatenate(ra, axis=1) + ba_ref[...])
    ig = _sigmoid(jnp.concatenate(ia, axis=1) + bx_ref[...])
    log_a = (-R_C) * r * _softplus(-lam_ref[...])
    a = jnp.exp(log_a)
    s = jnp.maximum(1.0 - a * a, 0.0)
    u = (s * lax.rsqrt(jnp.maximum(s, SQRT_FLOOR))) * (ig * xr)

    row8 = lax.broadcasted_iota(jnp.int32, (tb, R_WIDTH), 0) % SUBLANES
    for k in (1, 2, 4):
        keep = row8 >= k
        a_sh = jnp.where(keep, pltpu.roll(a, k, 0), 1.0)
        u_sh = jnp.where(keep, pltpu.roll(u, k, 0), 0.0)
        u = a * u_sh + u
        a = a * a_sh
    acum[...] = a
    ucum[...] = u

    def body(g, hprev):
        r0 = pl.multiple_of(g * SUBLANES, SUBLANES)
        h = acum[pl.ds(r0, SUBLANES), :] * hprev + ucum[pl.ds(r0, SUBLANES), :]
        ucum[pl.ds(r0, SUBLANES), :] = h
        return jnp.broadcast_to(h[SUBLANES - 1:SUBLANES, :], (SUBLANES, R_WIDTH))

    hlast = lax.fori_loop(0, tb // SUBLANES, body, hcarry[...])
    hcarry[...] = hlast
    o_ref[...] = ucum[...] * _silu(rz_ref[...])


def _rglru(proj, cw, cb, wa, ba, wx, bx, lam, batch, seq):
    tb = R_TB
    nblk = seq // tb
    T = batch * seq
    vec = lambda: pl.BlockSpec((1, R_WIDTH), lambda b, i: (0, 0))
    return pl.pallas_call(
        _rglru_kernel,
        grid=(batch, nblk),
        in_specs=[
            pl.BlockSpec((tb, R_WIDTH), lambda b, i: (b * nblk + i, COL_RX // R_WIDTH)),
            pl.BlockSpec((tb, R_WIDTH), lambda b, i: (b * nblk + i, COL_RZ // R_WIDTH)),
            pl.BlockSpec((CONV_WIDTH, R_WIDTH), lambda b, i: (0, 0)),
            vec(),
            pl.BlockSpec((R_BLOCKS, R_BLOCK_DIM, R_BLOCK_DIM), lambda b, i: (0, 0, 0)),
            vec(),
            pl.BlockSpec((R_BLOCKS, R_BLOCK_DIM, R_BLOCK_DIM), lambda b, i: (0, 0, 0)),
            vec(),
            vec(),
        ],
        out_specs=pl.BlockSpec((tb, R_WIDTH), lambda b, i: (b * nblk + i, 0)),
        out_shape=jax.ShapeDtypeStruct((T, R_WIDTH), F32),
        scratch_shapes=[
            pltpu.VMEM((SUBLANES + tb, R_WIDTH), F32),
            pltpu.VMEM((tb, R_WIDTH), F32),
            pltpu.VMEM((tb, R_WIDTH), F32),
            pltpu.VMEM((SUBLANES, R_WIDTH), F32),
        ],
        compiler_params=pltpu.CompilerParams(
            dimension_semantics=("parallel", "arbitrary"),
            vmem_limit_bytes=VMEM_LIMIT),
        name="rglru",
    )(proj, proj, cw, cb, wa, ba, wx, bx, lam)


G_TB = 256


def _l2norm(x):
    return x * lax.rsqrt(jnp.sum(x * x, axis=-1, keepdims=True) + RMS_EPS)


def _bdot(a, b):
    return jnp.dot(a.astype(BF16), b.astype(BF16), preferred_element_type=F32)


def _unit_lower_solves(ms, rhss):
    c = ms[0].shape[0]
    eye = (lax.broadcasted_iota(jnp.int32, (c, c), 0) ==
           lax.broadcasted_iota(jnp.int32, (c, c), 1)).astype(F32)
    p_s = [eye - m for m in ms]
    a_s = [_bdot(m, m) for m in ms]
    k = 2
    while k < c // 2:
        st = [_bdot(jnp.concatenate([a, p], axis=0), a) for a, p in zip(a_s, p_s)]
        p_s = [p + s[c:] for p, s in zip(p_s, st)]
        a_s = [s[:c] for s in st]
        k *= 2
    ys = [r + _bdot(a, r) for a, r in zip(a_s, rhss)]
    return [_bdot(p, y) for p, y in zip(p_s, ys)]


G_GROUP = 2


def _gdn_kernel(gq_ref, gk_ref, gv_ref, gz_ref, sm_ref, cwq_ref, cwk_ref, cwv_ref,
                alog_ref, dtb_ref, nw_ref, o_ref,
                qbuf, kbuf, vbuf, qs, ks, vs, bg, us, ws, qds, qks, kdts, gls, state):
    tb = G_TB
    C = G_CHUNK
    nb = gq_ref.shape[0]

    @pl.when(pl.program_id(0) == 0)
    def _():
        zeros = jnp.zeros((nb, SUBLANES, G_WIDTH), F32)
        qbuf[:, 0:SUBLANES, :] = zeros
        kbuf[:, 0:SUBLANES, :] = zeros
        vbuf[:, 0:SUBLANES, :] = zeros
        state[...] = jnp.zeros_like(state)

    lane = lax.broadcasted_iota(jnp.int32, (tb, LANES), 1)
    for b in range(nb):
        for src, buf, cw, dst in ((gq_ref, qbuf, cwq_ref, qs), (gk_ref, kbuf, cwk_ref, ks),
                                  (gv_ref, vbuf, cwv_ref, vs)):
            buf[b, SUBLANES:SUBLANES + tb, :] = src[b]
            dst[b] = _silu(_causal_conv(buf.at[b], cw[...], tb))
            buf[b, 0:SUBLANES, :] = buf[b, tb:tb + SUBLANES, :]
        small = sm_ref[b]
        g_all = -jnp.exp(alog_ref[...]) * _softplus(small + dtb_ref[...])
        bg[b] = jnp.where(lane < G_HEADS, _sigmoid(small), g_all)

    ri = lax.broadcasted_iota(jnp.int32, (C, C), 0)
    ci = lax.broadcasted_iota(jnp.int32, (C, C), 1)
    tril = ri >= ci
    strict = ri > ci
    tril_f = tril.astype(F32)

    def intra(grp, carry):
        dests, ms, rhss = [], [], []
        for b in range(nb):
            for cc in range(G_GROUP):
                c = grp * G_GROUP + cc
                r0 = pl.multiple_of(c * C, C)
                bgc = bg[b, pl.ds(r0, C), :]
                gcum_col = jnp.dot(tril_f, bgc, precision=HIGHEST, preferred_element_type=F32)
                gcum_row = gcum_col.T[:SUBLANES, :]
                for h in range(G_HEADS):
                    cols = slice(h * G_HEAD_DIM, (h + 1) * G_HEAD_DIM)
                    qn = _l2norm(qs[b, pl.ds(r0, C), cols]) * (G_HEAD_DIM ** -0.5)
                    kn = _l2norm(ks[b, pl.ds(r0, C), cols])
                    v = vs[b, pl.ds(r0, C), cols]
                    beta = jnp.broadcast_to(bgc[:, h:h + 1], (C, G_HEAD_DIM))
                    gc = jnp.broadcast_to(gcum_col[:, G_HEADS + h:G_HEADS + h + 1], (C, G_HEAD_DIM))
                    gr = gcum_row[G_HEADS + h:G_HEADS + h + 1, :]
                    g_last = gc[C - 1:C, :]
                    decay = jnp.where(tril, jnp.exp(gc[:, :C] - gr), 0.0)
                    eg = jnp.exp(gc)
                    knb = kn.astype(BF16)
                    qk_kk = lax.dot_general(jnp.concatenate([qn.astype(BF16), knb], axis=0), knb,
                                            (((1,), (1,)), ((), ())), preferred_element_type=F32)
                    qks[b, pl.ds(r0, C), h * C:(h + 1) * C] = (qk_kk[:C] * decay).astype(BF16)
                    ms.append(jnp.where(strict, qk_kk[C:] * beta[:, :C] * decay, 0.0))
                    qds[b, pl.ds(r0, C), cols] = (qn * eg).astype(BF16)
                    kdts[b, c * G_HEADS + h] = (kn * jnp.exp(g_last - gc)).T.astype(BF16)
                    gls[b, c * G_HEADS + h] = jnp.broadcast_to(jnp.exp(g_last), (SUBLANES, LANES))
                    rhss.append(jnp.concatenate([v * beta, kn * (beta * eg)], axis=1))
                    dests.append((b, r0, cols))
        for (b, r0, cols), uw in zip(dests, _unit_lower_solves(ms, rhss)):
            us[b, pl.ds(r0, C), cols] = uw[:, :G_HEAD_DIM]
            ws[b, pl.ds(r0, C), cols] = uw[:, G_HEAD_DIM:].astype(BF16)
        return carry

    lax.fori_loop(0, tb // (C * G_GROUP), intra, 0)

    chains = [(b, h) for b in range(nb) for h in range(G_HEADS)]

    def inter(c, carry):
        r0 = pl.multiple_of(c * C, C)
        rows = pl.ds(r0, C)
        hcols = lambda h: slice(h * G_HEAD_DIM, (h + 1) * G_HEAD_DIM)
        s_old = [state[b, h] for b, h in chains]
        sb = [s.astype(BF16) for s in s_old]
        w_s = [jnp.dot(ws[b, rows, hcols(h)], s, preferred_element_type=F32)
               for (b, h), s in zip(chains, sb)]
        vnb = [(us[b, rows, hcols(h)] - x).astype(BF16) for (b, h), x in zip(chains, w_s)]
        upd = [jnp.dot(kdts[b, c * G_HEADS + h], v, preferred_element_type=F32)
               for (b, h), v in zip(chains, vnb)]
        for (b, h), s, x in zip(chains, s_old, upd):
            state[b, h] = s * gls[b, c * G_HEADS + h][0:1, :] + x
        for (b, h), s, v in zip(chains, sb, vnb):
            o = (jnp.dot(qds[b, rows, hcols(h)], s, preferred_element_type=F32)
                 + jnp.dot(qks[b, rows, h * C:(h + 1) * C], v, preferred_element_type=F32))
            on = o * lax.rsqrt(jnp.mean(o * o, axis=-1, keepdims=True) + RMS_EPS) * nw_ref[...]
            o_ref[b, rows, hcols(h)] = on * _silu(gz_ref[b, rows, hcols(h)])
        return carry

    lax.fori_loop(0, tb // C, inter, 0)


def _gdn(proj, cwq, cwk, cwv, alog, dtb, nw, batch, seq):
    tb = G_TB
    nchunk = tb // G_CHUNK
    proj3 = proj.reshape(batch, seq, N_PROJ)
    seg = lambda col: pl.BlockSpec((batch, tb, G_WIDTH), lambda i: (0, i, col // G_WIDTH))
    cw = lambda: pl.BlockSpec((CONV_WIDTH, G_WIDTH), lambda i: (0, 0))
    vec = lambda: pl.BlockSpec((1, LANES), lambda i: (0, 0))
    out = pl.pallas_call(
        _gdn_kernel,
        grid=(seq // tb,),
        in_specs=[seg(COL_GQ), seg(COL_GK), seg(COL_GV), seg(COL_GZ),
                  pl.BlockSpec((batch, tb, LANES), lambda i: (0, i, COL_SMALL // LANES)),
                  cw(), cw(), cw(), vec(), vec(), vec()],
        out_specs=pl.BlockSpec((batch, tb, G_WIDTH), lambda i: (0, i, 0)),
        out_shape=jax.ShapeDtypeStruct((batch, seq, G_WIDTH), F32),
        scratch_shapes=[
            pltpu.VMEM((batch, SUBLANES + tb, G_WIDTH), F32),
            pltpu.VMEM((batch, SUBLANES + tb, G_WIDTH), F32),
            pltpu.VMEM((batch, SUBLANES + tb, G_WIDTH), F32),
            pltpu.VMEM((batch, tb, G_WIDTH), F32),
            pltpu.VMEM((batch, tb, G_WIDTH), F32),
            pltpu.VMEM((batch, tb, G_WIDTH), F32),
            pltpu.VMEM((batch, tb, LANES), F32),
            pltpu.VMEM((batch, tb, G_WIDTH), F32),
            pltpu.VMEM((batch, tb, G_WIDTH), BF16),
            pltpu.VMEM((batch, tb, G_WIDTH), BF16),
            pltpu.VMEM((batch, tb, G_HEADS * G_CHUNK), BF16),
            pltpu.VMEM((batch, nchunk * G_HEADS, G_HEAD_DIM, G_CHUNK), BF16),
            pltpu.VMEM((batch, nchunk * G_HEADS, SUBLANES, LANES), F32),
            pltpu.VMEM((batch, G_HEADS, G_HEAD_DIM, G_HEAD_DIM), F32),
        ],
        compiler_params=pltpu.CompilerParams(
            dimension_semantics=("arbitrary",),
            vmem_limit_bytes=VMEM_LIMIT),
        name="gated_deltanet",
    )(proj3, proj3, proj3, proj3, proj3, cwq, cwk, cwv, alog, dtb, nw)
    return out.reshape(batch * seq, G_WIDTH)


OUT_TM = 512


def _outproj_kernel(ya_ref, yr_ref, yg_ref, x_ref, w_ref, g_ref, b_ref, o_ref, ob_ref):
    y = jnp.dot(ya_ref[...].astype(BF16), w_ref[0:A_WIDTH, :], preferred_element_type=F32)
    y += jnp.dot(yr_ref[...].astype(BF16), w_ref[A_WIDTH:A_WIDTH + R_WIDTH, :],
                 preferred_element_type=F32)
    y += jnp.dot(yg_ref[...].astype(BF16), w_ref[A_WIDTH + R_WIDTH:MIX_WIDTH, :],
                 preferred_element_type=F32)
    z = DEEPNORM_ALPHA * x_ref[...] + y
    mu = jnp.mean(z, axis=-1, keepdims=True)
    zc = z - mu
    var = jnp.mean(zc * zc, axis=-1, keepdims=True)
    out = zc * lax.rsqrt(var + LN_EPS) * g_ref[...] + b_ref[...]
    o_ref[...] = out
    ob_ref[...] = out.astype(BF16)


def _outproj(ya, yr, yg, x, w, layer, g, b):
    T = x.shape[0]
    tm = OUT_TM
    row = lambda width: pl.BlockSpec((tm, width), lambda i: (i, 0))
    vec = lambda: pl.BlockSpec((1, D_MODEL), lambda i: (0, 0))
    return pl.pallas_call(
        _outproj_kernel,
        grid=(T // tm,),
        in_specs=[row(A_WIDTH), row(R_WIDTH), row(G_WIDTH), row(D_MODEL),
                  pl.BlockSpec((None, MIX_WIDTH, D_MODEL), lambda i: (layer, 0, 0),
                               pipeline_mode=pl.Buffered(1)),
                  vec(), vec()],
        out_specs=[row(D_MODEL), row(D_MODEL)],
        out_shape=[jax.ShapeDtypeStruct((T, D_MODEL), F32),
                   jax.ShapeDtypeStruct((T, D_MODEL), BF16)],
        compiler_params=pltpu.CompilerParams(
            dimension_semantics=("parallel",),
            vmem_limit_bytes=VMEM_LIMIT),
        name="outproj_deepnorm",
    )(ya, yr, yg, x, w, g, b)


def _rope_tables(seq):
    half = A_HEAD_DIM // 2
    inv = 1.0 / (ROPE_THETA ** (jnp.arange(0, A_HEAD_DIM, 2, dtype=F32) / A_HEAD_DIM))
    ang = jnp.arange(seq, dtype=F32)[:, None] * inv[None, :]
    cos, sin = jnp.cos(ang), jnp.sin(ang)
    reps = LANES // A_HEAD_DIM
    cos_t = jnp.tile(jnp.concatenate([cos, cos], axis=1), (1, reps))
    sin_t = jnp.tile(jnp.concatenate([-sin, sin], axis=1), (1, reps))
    assert cos_t.shape == (seq, LANES) and half * 2 == A_HEAD_DIM
    return cos_t, sin_t


def _lane_vec(v, offset):
    return jnp.zeros((1, LANES), F32).at[0, offset:offset + v.shape[0]].set(v.astype(F32))


@jax.jit
def _forward(x, w_in, sinks, r_conv_w, r_conv_b, r_wa, r_ba, r_wx, r_bx, r_lam,
             g_conv_w, g_a_log, g_dt_bias, g_norm_w, w_out, ln_g, ln_b):
    batch, seq, _ = x.shape
    T = batch * seq
    cos_t, sin_t = _rope_tables(seq)
    w_out_b = w_out.astype(BF16)
    w_in_t = jnp.swapaxes(w_in, 1, 2)
    xf = x.reshape(T, D_MODEL)
    xb = xf.astype(BF16)
    for l in range(DEPTH):
        proj = _inproj(xb, w_in_t, l)
        ya = _attention(proj, sinks[l], cos_t, sin_t, batch, seq)
        yr = _rglru(proj, r_conv_w[l], r_conv_b[l][None, :], r_wa[l].astype(BF16), r_ba[l][None, :],
                    r_wx[l].astype(BF16), r_bx[l][None, :], r_lam[l][None, :], batch, seq)
        gcw = g_conv_w[l]
        yg = _gdn(proj, gcw[:, :G_WIDTH], gcw[:, G_WIDTH:2 * G_WIDTH], gcw[:, 2 * G_WIDTH:],
                  _lane_vec(g_a_log[l], G_HEADS), _lane_vec(g_dt_bias[l], G_HEADS),
                  g_norm_w[l][None, :], batch, seq)
        xf, xb = _outproj(ya, yr, yg, xf, w_out_b, l, ln_g[l][None, :], ln_b[l][None, :])
    return xf.reshape(batch, seq, D_MODEL)


def kernel(x, w_in, sinks, r_conv_w, r_conv_b, r_wa, r_ba, r_wx, r_bx, r_lam, g_conv_w, g_a_log,
           g_dt_bias, g_norm_w, w_out, ln_g, ln_b):
    return _forward(x, w_in, sinks, r_conv_w, r_conv_b, r_wa, r_ba, r_wx, r_bx, r_lam,
                    g_conv_w, g_a_log, g_dt_bias, g_norm_w, w_out, ln_g, ln_b)
```

```python
import functools
import math

import numpy as np
import jax
import jax.numpy as jnp
from jax import lax
from jax.experimental import pallas as pl
from jax.experimental.pallas import tpu as pltpu

D_MODEL = 2048
DEPTH = 2
A_HEADS = 8
A_KV_HEADS = 2
A_HEAD_DIM = 64
A_WIDTH = A_HEADS * A_HEAD_DIM
A_KV_WIDTH = A_KV_HEADS * A_HEAD_DIM
WINDOW = 128
A_BLOCK = 128
ROPE_THETA = 10000.0
R_WIDTH = 1024
R_BLOCKS = 8
R_BLOCK_DIM = R_WIDTH // R_BLOCKS
R_C = 8.0
CONV_WIDTH = 4
G_HEADS = 4
G_HEAD_DIM = 128
G_WIDTH = G_HEADS * G_HEAD_DIM
G_CHUNK = 64
MIX_WIDTH = A_WIDTH + R_WIDTH + G_WIDTH
DEEPNORM_ALPHA = (2 * DEPTH) ** 0.25
LN_EPS = 1e-5
RMS_EPS = 1e-6

LANES = 128
SUBLANES = 8
VMEM_LIMIT = 56 * 1024 * 1024

COL_RX = 0
COL_RZ = 1024
COL_AQ = 2048
COL_AZ = 2560
COL_GQ = 3072
COL_GK = 3584
COL_GV = 4096
COL_GZ = 4608
COL_KV = 5120
COL_SMALL = 5376
N_PROJ = 5632

F32 = jnp.float32
BF16 = jnp.bfloat16
HIGHEST = lax.Precision.HIGHEST


def _sigmoid(x):
    return 0.5 * jnp.tanh(0.5 * x) + 0.5


def _silu(x):
    return x * _sigmoid(x)


def _softplus(x):
    return jnp.maximum(x, 0.0) + jnp.log1p(jnp.exp(-jnp.abs(x)))


N_IN = 5384
IN_TM = 2048
IN_TN = 512
IN_SUB = 256
IN_TILE_ORDER = (5, 6, 7, 8, 9, 10, 11, 12, 0, 1, 3, 4, 13, 14, 15, 16, 17, 18, 19, 20, 2, 21)


def _inproj_kernel(order_ref, x_ref, w0_ref, w1_ref, o_ref, wb):
    j = pl.program_id(0)

    @pl.when(pl.program_id(1) == 0)
    def _():
        row = lax.broadcasted_iota(jnp.int32, (IN_SUB, D_MODEL), 0)
        for t, w_ref in enumerate((w0_ref, w1_ref)):
            valid = N_IN - order_ref[2 * j + t] * IN_SUB
            wb[t * IN_SUB:(t + 1) * IN_SUB, :] = jnp.where(row < valid, w_ref[...], 0.0).astype(BF16)

    o_ref[...] = lax.dot_general(x_ref[...], wb[...], (((1,), (1,)), ((), ())),
                                 preferred_element_type=F32)


def _inproj(xb, w_in_t, layer):
    T = xb.shape[0]
    per_step = IN_TN // IN_SUB
    assert per_step == 2 and len(IN_TILE_ORDER) * IN_SUB == N_PROJ
    order = jnp.asarray(IN_TILE_ORDER, jnp.int32)
    w_spec = lambda t: pl.BlockSpec((None, IN_SUB, D_MODEL),
                                    lambda j, i, order_ref: (layer, order_ref[per_step * j + t], 0))
    return pl.pallas_call(
        _inproj_kernel,
        grid_spec=pltpu.PrefetchScalarGridSpec(
            num_scalar_prefetch=1,
            grid=(N_PROJ // IN_TN, T // IN_TM),
            in_specs=[pl.BlockSpec((IN_TM, D_MODEL), lambda j, i, order_ref: (i, 0)),
                      w_spec(0), w_spec(1)],
            out_specs=pl.BlockSpec((IN_TM, IN_TN), lambda j, i, order_ref: (i, j)),
            scratch_shapes=[pltpu.VMEM((IN_TN, D_MODEL), BF16)]),
        out_shape=jax.ShapeDtypeStruct((T, N_PROJ), F32),
        compiler_params=pltpu.CompilerParams(
            dimension_semantics=("arbitrary", "arbitrary"),
            vmem_limit_bytes=VMEM_LIMIT),
        name="inproj",
    )(order, xb, w_in_t, w_in_t)


def _rope(x, cos, sin_signed):
    w = x.shape[1]
    reps = w // LANES
    if reps > 1:
        cos = jnp.concatenate([cos] * reps, axis=1)
        sin_signed = jnp.concatenate([sin_signed] * reps, axis=1)
    lane = lax.broadcasted_iota(jnp.int32, x.shape, 1)
    first_half = (lane % A_HEAD_DIM) < (A_HEAD_DIM // 2)
    swapped = jnp.where(first_half,
                        pltpu.roll(x, w - A_HEAD_DIM // 2, 1),
                        pltpu.roll(x, A_HEAD_DIM // 2, 1))
    return x * cos + swapped * sin_signed


A_TQ = 256
A_HALF = LANES // 2


def _attn_kernel(sinks_ref, q_ref, kv_ref, az_ref, cos_ref, sin_ref, o_ref, prev):
    step = pl.program_id(1)
    nblk = A_WIDTH // LANES

    @pl.when(step == 0)
    def _():
        prev[...] = jnp.zeros_like(prev)

    cos = cos_ref[...]
    sin = sin_ref[...]
    q = (_rope(q_ref[...], cos, sin) * (A_HEAD_DIM ** -0.5)).astype(BF16)
    kv = kv_ref[...]
    k = _rope(kv[:, :A_KV_WIDTH], cos, sin)
    v = kv[:, A_KV_WIDTH:]
    kb = k.astype(BF16)
    vb = v.astype(BF16)
    kswb = pltpu.roll(k, A_HALF, 1).astype(BF16)
    vswb = pltpu.roll(v, A_HALF, 1).astype(BF16)

    rows4 = nblk * A_BLOCK
    qi = lax.broadcasted_iota(jnp.int32, (rows4, 2 * A_BLOCK), 0) % A_BLOCK
    kj = lax.broadcasted_iota(jnp.int32, (rows4, 2 * A_BLOCK), 1)
    diff = qi - kj + A_BLOCK
    band = (diff >= 0) & (diff < WINDOW)
    slab = lax.broadcasted_iota(jnp.int32, (rows4, 1), 0) // A_BLOCK
    lo = lax.broadcasted_iota(jnp.int32, (A_BLOCK, LANES), 1) < A_HALF
    ones = jnp.ones((2 * A_BLOCK, LANES), BF16)

    def softmax_pv(lhs, k_all, v_aug, heads, mask):
        s = lax.dot_general(lhs, k_all, (((1,), (1,)), ((), ())), preferred_element_type=F32)
        s = jnp.where(mask, s, -jnp.inf)
        sink = jnp.zeros((rows4, 1), F32)
        for n, h in enumerate(heads):
            sink = jnp.where(slab == n, sinks_ref[h], sink)
        m = jnp.maximum(jnp.max(s, axis=-1, keepdims=True), sink)
        p = jnp.exp(s - m).astype(BF16)
        o = jnp.dot(p, v_aug, preferred_element_type=F32)
        den = o[:, LANES:] + jnp.exp(sink - m)
        return o[:, :LANES] / den

    kp, kpsw, vp, vpsw = prev[0], prev[1], prev[2], prev[3]
    for j in range(A_TQ // A_BLOCK):
        rows = slice(j * A_BLOCK, (j + 1) * A_BLOCK)
        kc, kcsw, vc, vcsw = kb[rows], kswb[rows], vb[rows], vswb[rows]
        k_all = jnp.concatenate([kp, kc], axis=0)
        k_all_sw = jnp.concatenate([kpsw, kcsw], axis=0)
        v_aug = jnp.concatenate([jnp.concatenate([vp, vc], axis=0), ones], axis=1)
        v_aug_sw = jnp.concatenate([jnp.concatenate([vpsw, vcsw], axis=0), ones], axis=1)
        mask = band if j > 0 else band & ((kj >= A_BLOCK) | (step > 0))

        qj = q[rows]
        blocks = [qj[:, n * LANES:(n + 1) * LANES] for n in range(nblk)]
        zero = jnp.zeros_like(blocks[0])
        q_lo = [jnp.where(lo, x, zero) for x in blocks]
        q_hi = [jnp.where(lo, zero, x) for x in blocks]
        half = nblk // A_KV_HEADS
        lhs_a = jnp.concatenate(q_lo[:half] + q_hi[half:], axis=0)
        lhs_b = jnp.concatenate(q_hi[:half] + q_lo[half:], axis=0)
        heads_a = [2 * n for n in range(half)] + [2 * n + 1 for n in range(half, nblk)]
        heads_b = [2 * n + 1 for n in range(half)] + [2 * n for n in range(half, nblk)]
        oa = softmax_pv(lhs_a, k_all, v_aug, heads_a, mask)
        ob = softmax_pv(lhs_b, k_all_sw, v_aug_sw, heads_b, mask)
        outs = []
        for n in range(nblk):
            a_n = oa[n * A_BLOCK:(n + 1) * A_BLOCK]
            b_n = ob[n * A_BLOCK:(n + 1) * A_BLOCK]
            outs.append(jnp.where(lo, a_n, b_n) if n < half else jnp.where(lo, b_n, a_n))
        out = jnp.concatenate(outs, axis=1)
        o_ref[rows, :] = out * _silu(az_ref[rows, :])
        kp, kpsw, vp, vpsw = kc, kcsw, vc, vcsw

    prev[0] = kp
    prev[1] = kpsw
    prev[2] = vp
    prev[3] = vpsw


def _attention(proj, sinks, cos_t, sin_t, batch, seq):
    nq = seq // A_TQ
    T = batch * seq
    assert (A_HEAD_DIM * 2 == LANES and A_KV_WIDTH == LANES
            and (A_HEADS // A_KV_HEADS) % 2 == 0 and WINDOW == A_BLOCK)
    return pl.pallas_call(
        _attn_kernel,
        grid=(batch, nq),
        in_specs=[
            pl.BlockSpec(memory_space=pltpu.SMEM),
            pl.BlockSpec((A_TQ, A_WIDTH), lambda b, i: (b * nq + i, COL_AQ // A_WIDTH)),
            pl.BlockSpec((A_TQ, 2 * A_KV_WIDTH), lambda b, i: (b * nq + i, COL_KV // (2 * A_KV_WIDTH))),
            pl.BlockSpec((A_TQ, A_WIDTH), lambda b, i: (b * nq + i, COL_AZ // A_WIDTH)),
            pl.BlockSpec((A_TQ, LANES), lambda b, i: (i, 0)),
            pl.BlockSpec((A_TQ, LANES), lambda b, i: (i, 0)),
        ],
        out_specs=pl.BlockSpec((A_TQ, A_WIDTH), lambda b, i: (b * nq + i, 0)),
        out_shape=jax.ShapeDtypeStruct((T, A_WIDTH), F32),
        scratch_shapes=[pltpu.VMEM((4, A_BLOCK, LANES), BF16)],
        compiler_params=pltpu.CompilerParams(
            dimension_semantics=("parallel", "arbitrary"),
            vmem_limit_bytes=VMEM_LIMIT),
        name="swa_attention",
    )(sinks, proj, proj, proj, cos_t, sin_t)


R_TB = 256
SQRT_FLOOR = 1e-30


def _causal_conv(buf_ref, w, tb):
    xext = buf_ref[0:SUBLANES + tb, :]
    acc = xext[SUBLANES:, :] * w[CONV_WIDTH - 1:CONV_WIDTH, :]
    for k in range(CONV_WIDTH - 1):
        shifted = pltpu.roll(xext, CONV_WIDTH - 1 - k, 0)[SUBLANES:, :]
        acc = acc + shifted * w[k:k + 1, :]
    return acc


def _rglru_kernel(rx_ref, rz_ref, cw_ref, cb_ref, wa_ref, ba_ref, wx_ref, bx_ref, lam_ref,
                  o_ref, xbuf, acum, ucum, hcarry):
    tb = R_TB

    @pl.when(pl.program_id(1) == 0)
    def _():
        xbuf[0:SUBLANES, :] = jnp.zeros((SUBLANES, R_WIDTH), F32)
        hcarry[...] = jnp.zeros_like(hcarry)

    xbuf[SUBLANES:SUBLANES + tb, :] = rx_ref[...]
    xr = _causal_conv(xbuf, cw_ref[...], tb) + cb_ref[...]
    xbuf[0:SUBLANES, :] = xbuf[tb:tb + SUBLANES, :]

    xrb = xr.astype(BF16)
    ra, ia = [], []
    for n in range(R_BLOCKS):
        xb = xrb[:, n * R_BLOCK_DIM:(n + 1) * R_BLOCK_DIM]
        ra.append(jnp.dot(xb, wa_ref[n], preferred_element_type=F32))
        ia.append(jnp.dot(xb, wx_ref[n], preferred_element_type=F32))
    r = _sigmoid(jnp.concatenate(ra, axis=1) + ba_ref[...])
    ig = _sigmoid(jnp.concatenate(ia, axis=1) + bx_ref[...])
    log_a = (-R_C) * r * _softplus(-lam_ref[...])
    a = jnp.exp(log_a)
    s = jnp.maximum(1.0 - a * a, 0.0)
    u = (s * lax.rsqrt(jnp.maximum(s, SQRT_FLOOR))) * (ig * xr)

    row8 = lax.broadcasted_iota(jnp.int32, (tb, R_WIDTH), 0) % SUBLANES
    for k in (1, 2, 4):
        keep = row8 >= k
        a_sh = jnp.where(keep, pltpu.roll(a, k, 0), 1.0)
        u_sh = jnp.where(keep, pltpu.roll(u, k, 0), 0.0)
        u = a * u_sh + u
        a = a * a_sh
    acum[...] = a
    ucum[...] = u

    def body(g, hprev):
        r0 = pl.multiple_of(g * SUBLANES, SUBLANES)
        h = acum[pl.ds(r0, SUBLANES), :] * hprev + ucum[pl.ds(r0, SUBLANES), :]
        ucum[pl.ds(r0, SUBLANES), :] = h
        return jnp.broadcast_to(h[SUBLANES - 1:SUBLANES, :], (SUBLANES, R_WIDTH))

    hlast = lax.fori_loop(0, tb // SUBLANES, body, hcarry[...])
    hcarry[...] = hlast
    o_ref[...] = ucum[...] * _silu(rz_ref[...])


def _rglru(proj, cw, cb, wa, ba, wx, bx, lam, batch, seq):
    tb = R_TB
    nblk = seq // tb
    T = batch * seq
    vec = lambda: pl.BlockSpec((1, R_WIDTH), lambda b, i: (0, 0))
    return pl.pallas_call(
        _rglru_kernel,
        grid=(batch, nblk),
        in_specs=[
            pl.BlockSpec((tb, R_WIDTH), lambda b, i: (b * nblk + i, COL_RX // R_WIDTH)),
            pl.BlockSpec((tb, R_WIDTH), lambda b, i: (b * nblk + i, COL_RZ // R_WIDTH)),
            pl.BlockSpec((CONV_WIDTH, R_WIDTH), lambda b, i: (0, 0)),
            vec(),
            pl.BlockSpec((R_BLOCKS, R_BLOCK_DIM, R_BLOCK_DIM), lambda b, i: (0, 0, 0)),
            vec(),
            pl.BlockSpec((R_BLOCKS, R_BLOCK_DIM, R_BLOCK_DIM), lambda b, i: (0, 0, 0)),
            vec(),
            vec(),
        ],
        out_specs=pl.BlockSpec((tb, R_WIDTH), lambda b, i: (b * nblk + i, 0)),
        out_shape=jax.ShapeDtypeStruct((T, R_WIDTH), F32),
        scratch_shapes=[
            pltpu.VMEM((SUBLANES + tb, R_WIDTH), F32),
            pltpu.VMEM((tb, R_WIDTH), F32),
            pltpu.VMEM((tb, R_WIDTH), F32),
            pltpu.VMEM((SUBLANES, R_WIDTH), F32),
        ],
        compiler_params=pltpu.CompilerParams(
            dimension_semantics=("parallel", "arbitrary"),
            vmem_limit_bytes=VMEM_LIMIT),
        name="rglru",
    )(proj, proj, cw, cb, wa, ba, wx, bx, lam)


G_TB = 256


def _l2norm(x):
    return x * lax.rsqrt(jnp.sum(x * x, axis=-1, keepdims=True) + RMS_EPS)


def _bdot(a, b):
    return jnp.dot(a.astype(BF16), b.astype(BF16), preferred_element_type=F32)


def _unit_lower_solves(ms, rhss):
    c = ms[0].shape[0]
    eye = (lax.broadcasted_iota(jnp.int32, (c, c), 0) ==
           lax.broadcasted_iota(jnp.int32, (c, c), 1)).astype(F32)
    p_s = [eye - m for m in ms]
    a_s = [_bdot(m, m) for m in ms]
    k = 2
    while k < c // 2:
        st = [_bdot(jnp.concatenate([a, p], axis=0), a) for a, p in zip(a_s, p_s)]
        p_s = [p + s[c:] for p, s in zip(p_s, st)]
        a_s = [s[:c] for s in st]
        k *= 2
    ys = [r + _bdot(a, r) for a, r in zip(a_s, rhss)]
    return [_bdot(p, y) for p, y in zip(p_s, ys)]


G_GROUP = 4


def _gdn_kernel(gq_ref, gk_ref, gv_ref, gz_ref, sm_ref, cwq_ref, cwk_ref, cwv_ref,
                alog_ref, dtb_ref, nw_ref, o_ref,
                qbuf, kbuf, vbuf, qs, ks, vs, bg, us, ws, qds, qks, kdts, gls, state):
    tb = G_TB
    C = G_CHUNK
    nb = gq_ref.shape[0]

    @pl.when(pl.program_id(0) == 0)
    def _():
        zeros = jnp.zeros((nb, SUBLANES, G_WIDTH), F32)
        qbuf[:, 0:SUBLANES, :] = zeros
        kbuf[:, 0:SUBLANES, :] = zeros
        vbuf[:, 0:SUBLANES, :] = zeros
        state[...] = jnp.zeros_like(state)

    lane = lax.broadcasted_iota(jnp.int32, (tb, LANES), 1)
    for b in range(nb):
        for src, buf, cw, dst in ((gq_ref, qbuf, cwq_ref, qs), (gk_ref, kbuf, cwk_ref, ks),
                                  (gv_ref, vbuf, cwv_ref, vs)):
            buf[b, SUBLANES:SUBLANES + tb, :] = src[b]
            dst[b] = _silu(_causal_conv(buf.at[b], cw[...], tb))
            buf[b, 0:SUBLANES, :] = buf[b, tb:tb + SUBLANES, :]
        small = sm_ref[b]
        g_all = -jnp.exp(alog_ref[...]) * _softplus(small + dtb_ref[...])
        bg[b] = jnp.where(lane < G_HEADS, _sigmoid(small), g_all)

    ri = lax.broadcasted_iota(jnp.int32, (C, C), 0)
    ci = lax.broadcasted_iota(jnp.int32, (C, C), 1)
    tril = ri >= ci
    strict = ri > ci
    tril_f = tril.astype(F32)

    def intra(grp, carry):
        dests, ms, rhss = [], [], []
        for b in range(nb):
            for cc in range(G_GROUP):
                c = grp * G_GROUP + cc
                r0 = pl.multiple_of(c * C, C)
                bgc = bg[b, pl.ds(r0, C), :]
                gcum_col = jnp.dot(tril_f, bgc, precision=HIGHEST, preferred_element_type=F32)
                gcum_row = gcum_col.T[:SUBLANES, :]
                for h in range(G_HEADS):
                    cols = slice(h * G_HEAD_DIM, (h + 1) * G_HEAD_DIM)
                    qn = _l2norm(qs[b, pl.ds(r0, C), cols]) * (G_HEAD_DIM ** -0.5)
                    kn = _l2norm(ks[b, pl.ds(r0, C), cols])
                    v = vs[b, pl.ds(r0, C), cols]
                    beta = jnp.broadcast_to(bgc[:, h:h + 1], (C, G_HEAD_DIM))
                    gc = jnp.broadcast_to(gcum_col[:, G_HEADS + h:G_HEADS + h + 1], (C, G_HEAD_DIM))
                    gr = gcum_row[G_HEADS + h:G_HEADS + h + 1, :]
                    g_last = gc[C - 1:C, :]
                    decay = jnp.where(tril, jnp.exp(gc[:, :C] - gr), 0.0)
                    eg = jnp.exp(gc)
                    knb = kn.astype(BF16)
                    qk_kk = lax.dot_general(jnp.concatenate([qn.astype(BF16), knb], axis=0), knb,
                                            (((1,), (1,)), ((), ())), preferred_element_type=F32)
                    qks[b, pl.ds(r0, C), h * C:(h + 1) * C] = (qk_kk[:C] * decay).astype(BF16)
                    ms.append(jnp.where(strict, qk_kk[C:] * beta[:, :C] * decay, 0.0))
                    qds[b, pl.ds(r0, C), cols] = (qn * eg).astype(BF16)
                    kdts[b, c * G_HEADS + h] = (kn * jnp.exp(g_last - gc)).T.astype(BF16)
                    gls[b, c * G_HEADS + h] = jnp.broadcast_to(jnp.exp(g_last), (SUBLANES, LANES))
                    rhss.append(jnp.concatenate([v * beta, kn * (beta * eg)], axis=1))
                    dests.append((b, r0, cols))
        for (b, r0, cols), uw in zip(dests, _unit_lower_solves(ms, rhss)):
            us[b, pl.ds(r0, C), cols] = uw[:, :G_HEAD_DIM]
            ws[b, pl.ds(r0, C), cols] = uw[:, G_HEAD_DIM:].astype(BF16)
        return carry

    lax.fori_loop(0, tb // (C * G_GROUP), intra, 0)

    chains = [(b, h) for b in range(nb) for h in range(G_HEADS)]

    def inter(c, carry):
        r0 = pl.multiple_of(c * C, C)
        rows = pl.ds(r0, C)
        hcols = lambda h: slice(h * G_HEAD_DIM, (h + 1) * G_HEAD_DIM)
        s_old = [state[b, h] for b, h in chains]
        sb = [s.astype(BF16) for s in s_old]
        w_s = [jnp.dot(ws[b, rows, hcols(h)], s, preferred_element_type=F32)
               for (b, h), s in zip(chains, sb)]
        vnb = [(us[b, rows, hcols(h)] - x).astype(BF16) for (b, h), x in zip(chains, w_s)]
        upd = [jnp.dot(kdts[b, c * G_HEADS + h], v, preferred_element_type=F32)
               for (b, h), v in zip(chains, vnb)]
        for (b, h), s, x in zip(chains, s_old, upd):
            state[b, h] = s * gls[b, c * G_HEADS + h][0:1, :] + x
        for (b, h), s, v in zip(chains, sb, vnb):
            o = (jnp.dot(qds[b, rows, hcols(h)], s, preferred_element_type=F32)
                 + jnp.dot(qks[b, rows, h * C:(h + 1) * C], v, preferred_element_type=F32))
            on = o * lax.rsqrt(jnp.mean(o * o, axis=-1, keepdims=True) + RMS_EPS) * nw_ref[...]
            o_ref[b, rows, hcols(h)] = on * _silu(gz_ref[b, rows, hcols(h)])
        return carry

    lax.fori_loop(0, tb // C, inter, 0, unroll=True)


def _gdn(proj, cwq, cwk, cwv, alog, dtb, nw, batch, seq):
    tb = G_TB
    nchunk = tb // G_CHUNK
    proj3 = proj.reshape(batch, seq, N_PROJ)
    seg = lambda col: pl.BlockSpec((batch, tb, G_WIDTH), lambda i: (0, i, col // G_WIDTH))
    cw = lambda: pl.BlockSpec((CONV_WIDTH, G_WIDTH), lambda i: (0, 0))
    vec = lambda: pl.BlockSpec((1, LANES), lambda i: (0, 0))
    out = pl.pallas_call(
        _gdn_kernel,
        grid=(seq // tb,),
        in_specs=[seg(COL_GQ), seg(COL_GK), seg(COL_GV), seg(COL_GZ),
                  pl.BlockSpec((batch, tb, LANES), lambda i: (0, i, COL_SMALL // LANES)),
                  cw(), cw(), cw(), vec(), vec(), vec()],
        out_specs=pl.BlockSpec((batch, tb, G_WIDTH), lambda i: (0, i, 0)),
        out_shape=jax.ShapeDtypeStruct((batch, seq, G_WIDTH), F32),
        scratch_shapes=[
            pltpu.VMEM((batch, SUBLANES + tb, G_WIDTH), F32),
            pltpu.VMEM((batch, SUBLANES + tb, G_WIDTH), F32),
            pltpu.VMEM((batch, SUBLANES + tb, G_WIDTH), F32),
            pltpu.VMEM((batch, tb, G_WIDTH), F32),
            pltpu.VMEM((batch, tb, G_WIDTH), F32),
            pltpu.VMEM((batch, tb, G_WIDTH), F32),
            pltpu.VMEM((batch, tb, LANES), F32),
            pltpu.VMEM((batch, tb, G_WIDTH), F32),
            pltpu.VMEM((batch, tb, G_WIDTH), BF16),
            pltpu.VMEM((batch, tb, G_WIDTH), BF16),
            pltpu.VMEM((batch, tb, G_HEADS * G_CHUNK), BF16),
            pltpu.VMEM((batch, nchunk * G_HEADS, G_HEAD_DIM, G_CHUNK), BF16),
            pltpu.VMEM((batch, nchunk * G_HEADS, SUBLANES, LANES), F32),
            pltpu.VMEM((batch, G_HEADS, G_HEAD_DIM, G_HEAD_DIM), F32),
        ],
        compiler_params=pltpu.CompilerParams(
            dimension_semantics=("arbitrary",),
            vmem_limit_bytes=VMEM_LIMIT),
        name="gated_deltanet",
    )(proj3, proj3, proj3, proj3, proj3, cwq, cwk, cwv, alog, dtb, nw)
    return out.reshape(batch * seq, G_WIDTH)


OUT_TM = 512


def _outproj_kernel(ya_ref, yr_ref, yg_ref, x_ref, w_ref, g_ref, b_ref, o_ref, ob_ref):
    y = jnp.dot(ya_ref[...].astype(BF16), w_ref[0:A_WIDTH, :], preferred_element_type=F32)
    y += jnp.dot(yr_ref[...].astype(BF16), w_ref[A_WIDTH:A_WIDTH + R_WIDTH, :],
                 preferred_element_type=F32)
    y += jnp.dot(yg_ref[...].astype(BF16), w_ref[A_WIDTH + R_WIDTH:MIX_WIDTH, :],
                 preferred_element_type=F32)
    z = DEEPNORM_ALPHA * x_ref[...] + y
    mu = jnp.mean(z, axis=-1, keepdims=True)
    zc = z - mu
    var = jnp.mean(zc * zc, axis=-1, keepdims=True)
    out = zc * lax.rsqrt(var + LN_EPS) * g_ref[...] + b_ref[...]
    o_ref[...] = out
    ob_ref[...] = out.astype(BF16)


def _outproj(ya, yr, yg, x, w, layer, g, b):
    T = x.shape[0]
    tm = OUT_TM
    row = lambda width: pl.BlockSpec((tm, width), lambda i: (i, 0))
    vec = lambda: pl.BlockSpec((1, D_MODEL), lambda i: (0, 0))
    return pl.pallas_call(
        _outproj_kernel,
        grid=(T // tm,),
        in_specs=[row(A_WIDTH), row(R_WIDTH), row(G_WIDTH), row(D_MODEL),
                  pl.BlockSpec((None, MIX_WIDTH, D_MODEL), lambda i: (layer, 0, 0),
                               pipeline_mode=pl.Buffered(1)),
                  vec(), vec()],
        out_specs=[row(D_MODEL), row(D_MODEL)],
        out_shape=[jax.ShapeDtypeStruct((T, D_MODEL), F32),
                   jax.ShapeDtypeStruct((T, D_MODEL), BF16)],
        compiler_params=pltpu.CompilerParams(
            dimension_semantics=("parallel",),
            vmem_limit_bytes=VMEM_LIMIT),
        name="outproj_deepnorm",
    )(ya, yr, yg, x, w, g, b)


def _rope_tables(seq):
    half = A_HEAD_DIM // 2
    inv = 1.0 / (ROPE_THETA ** (jnp.arange(0, A_HEAD_DIM, 2, dtype=F32) / A_HEAD_DIM))
    ang = jnp.arange(seq, dtype=F32)[:, None] * inv[None, :]
    cos, sin = jnp.cos(ang), jnp.sin(ang)
    reps = LANES // A_HEAD_DIM
    cos_t = jnp.tile(jnp.concatenate([cos, cos], axis=1), (1, reps))
    sin_t = jnp.tile(jnp.concatenate([-sin, sin], axis=1), (1, reps))
    assert cos_t.shape == (seq, LANES) and half * 2 == A_HEAD_DIM
    return cos_t, sin_t


def _lane_vec(v, offset):
    return jnp.zeros((1, LANES), F32).at[0, offset:offset + v.shape[0]].set(v.astype(F32))


@jax.jit
def _forward(x, w_in, sinks, r_conv_w, r_conv_b, r_wa, r_ba, r_wx, r_bx, r_lam,
             g_conv_w, g_a_log, g_dt_bias, g_norm_w, w_out, ln_g, ln_b):
    batch, seq, _ = x.shape
    T = batch * seq
    cos_t, sin_t = _rope_tables(seq)
    w_out_b = w_out.astype(BF16)
    w_in_t = jnp.swapaxes(w_in, 1, 2)
    xf = x.reshape(T, D_MODEL)
    xb = xf.astype(BF16)
    for l in range(DEPTH):
        proj = _inproj(xb, w_in_t, l)
        ya = _attention(proj, sinks[l], cos_t, sin_t, batch, seq)
        yr = _rglru(proj, r_conv_w[l], r_conv_b[l][None, :], r_wa[l].astype(BF16), r_ba[l][None, :],
                    r_wx[l].astype(BF16), r_bx[l][None, :], r_lam[l][None, :], batch, seq)
        gcw = g_conv_w[l]
        yg = _gdn(proj, gcw[:, :G_WIDTH], gcw[:, G_WIDTH:2 * G_WIDTH], gcw[:, 2 * G_WIDTH:],
                  _lane_vec(g_a_log[l], G_HEADS), _lane_vec(g_dt_bias[l], G_HEADS),
                  g_norm_w[l][None, :], batch, seq)
        xf, xb = _outproj(ya, yr, yg, xf, w_out_b, l, ln_g[l][None, :], ln_b[l][None, :])
    return xf.reshape(batch, seq, D_MODEL)


def kernel(x, w_in, sinks, r_conv_w, r_conv_b, r_wa, r_ba, r_wx, r_bx, r_lam, g_conv_w, g_a_log,
           g_dt_bias, g_norm_w, w_out, ln_g, ln_b):
    return _forward(x, w_in, sinks, r_conv_w, r_conv_b, r_wa, r_ba, r_wx, r_bx, r_lam,
                    g_conv_w, g_a_log, g_dt_bias, g_norm_w, w_out, ln_g, ln_b)
```

```python
import functools
import math

import numpy as np
import jax
import jax.numpy as jnp
from jax import lax
from jax.experimental import pallas as pl
from jax.experimental.pallas import tpu as pltpu

D_MODEL = 2048
DEPTH = 2
A_HEADS = 8
A_KV_HEADS = 2
A_HEAD_DIM = 64
A_WIDTH = A_HEADS * A_HEAD_DIM
A_KV_WIDTH = A_KV_HEADS * A_HEAD_DIM
WINDOW = 128
A_BLOCK = 128
ROPE_THETA = 10000.0
R_WIDTH = 1024
R_BLOCKS = 8
R_BLOCK_DIM = R_WIDTH // R_BLOCKS
R_C = 8.0
CONV_WIDTH = 4
G_HEADS = 4
G_HEAD_DIM = 128
G_WIDTH = G_HEADS * G_HEAD_DIM
G_CHUNK = 64
MIX_WIDTH = A_WIDTH + R_WIDTH + G_WIDTH
DEEPNORM_ALPHA = (2 * DEPTH) ** 0.25
LN_EPS = 1e-5
RMS_EPS = 1e-6

LANES = 128
SUBLANES = 8
VMEM_LIMIT = 56 * 1024 * 1024

COL_RX = 0
COL_RZ = 1024
COL_AQ = 2048
COL_AZ = 2560
COL_GQ = 3072
COL_GK = 3584
COL_GV = 4096
COL_GZ = 4608
COL_KV = 5120
COL_SMALL = 5376
N_PROJ = 5632

F32 = jnp.float32
BF16 = jnp.bfloat16
HIGHEST = lax.Precision.HIGHEST


def _sigmoid(x):
    return 0.5 * jnp.tanh(0.5 * x) + 0.5


def _silu(x):
    return x * _sigmoid(x)


def _softplus(x):
    return jnp.maximum(x, 0.0) + jnp.log1p(jnp.exp(-jnp.abs(x)))


N_IN = 5384
IN_TM = 2048
IN_TN = 512
IN_SUB = 256
IN_TILE_ORDER = (5, 6, 7, 8, 9, 10, 11, 12, 0, 1, 3, 4, 13, 14, 15, 16, 17, 18, 19, 20, 2, 21)


def _inproj_kernel(order_ref, x_ref, w0_ref, w1_ref, o_ref, wb):
    j = pl.program_id(0)

    @pl.when(pl.program_id(1) == 0)
    def _():
        row = lax.broadcasted_iota(jnp.int32, (IN_SUB, D_MODEL), 0)
        for t, w_ref in enumerate((w0_ref, w1_ref)):
            valid = N_IN - order_ref[2 * j + t] * IN_SUB
            wb[t * IN_SUB:(t + 1) * IN_SUB, :] = jnp.where(row < valid, w_ref[...], 0.0).astype(BF16)

    o_ref[...] = lax.dot_general(x_ref[...], wb[...], (((1,), (1,)), ((), ())),
                                 preferred_element_type=F32)


def _inproj(xb, w_in_t, layer):
    T = xb.shape[0]
    per_step = IN_TN // IN_SUB
    assert per_step == 2 and len(IN_TILE_ORDER) * IN_SUB == N_PROJ
    order = jnp.asarray(IN_TILE_ORDER, jnp.int32)
    w_spec = lambda t: pl.BlockSpec((None, IN_SUB, D_MODEL),
                                    lambda j, i, order_ref: (layer, order_ref[per_step * j + t], 0))
    return pl.pallas_call(
        _inproj_kernel,
        grid_spec=pltpu.PrefetchScalarGridSpec(
            num_scalar_prefetch=1,
            grid=(N_PROJ // IN_TN, T // IN_TM),
            in_specs=[pl.BlockSpec((IN_TM, D_MODEL), lambda j, i, order_ref: (i, 0)),
                      w_spec(0), w_spec(1)],
            out_specs=pl.BlockSpec((IN_TM, IN_TN), lambda j, i, order_ref: (i, j)),
            scratch_shapes=[pltpu.VMEM((IN_TN, D_MODEL), BF16)]),
        out_shape=jax.ShapeDtypeStruct((T, N_PROJ), F32),
        compiler_params=pltpu.CompilerParams(
            dimension_semantics=("arbitrary", "arbitrary"),
            vmem_limit_bytes=VMEM_LIMIT),
        name="inproj",
    )(order, xb, w_in_t, w_in_t)


def _rope(x, cos, sin_signed):
    w = x.shape[1]
    reps = w // LANES
    if reps > 1:
        cos = jnp.concatenate([cos] * reps, axis=1)
        sin_signed = jnp.concatenate([sin_signed] * reps, axis=1)
    lane = lax.broadcasted_iota(jnp.int32, x.shape, 1)
    first_half = (lane % A_HEAD_DIM) < (A_HEAD_DIM // 2)
    swapped = jnp.where(first_half,
                        pltpu.roll(x, w - A_HEAD_DIM // 2, 1),
                        pltpu.roll(x, A_HEAD_DIM // 2, 1))
    return x * cos + swapped * sin_signed


A_TQ = 128
A_HALF = LANES // 2


def _attn_kernel(sinks_ref, q_ref, kv_ref, az_ref, cos_ref, sin_ref, o_ref, prev):
    step = pl.program_id(1)
    nblk = A_WIDTH // LANES

    @pl.when(step == 0)
    def _():
        prev[...] = jnp.zeros_like(prev)

    cos = cos_ref[...]
    sin = sin_ref[...]
    q = (_rope(q_ref[...], cos, sin) * (A_HEAD_DIM ** -0.5)).astype(BF16)
    kv = kv_ref[...]
    k = _rope(kv[:, :A_KV_WIDTH], cos, sin)
    v = kv[:, A_KV_WIDTH:]
    kb = k.astype(BF16)
    vb = v.astype(BF16)
    kswb = pltpu.roll(k, A_HALF, 1).astype(BF16)
    vswb = pltpu.roll(v, A_HALF, 1).astype(BF16)

    rows4 = nblk * A_BLOCK
    qi = lax.broadcasted_iota(jnp.int32, (rows4, 2 * A_BLOCK), 0) % A_BLOCK
    kj = lax.broadcasted_iota(jnp.int32, (rows4, 2 * A_BLOCK), 1)
    diff = qi - kj + A_BLOCK
    band = (diff >= 0) & (diff < WINDOW)
    slab = lax.broadcasted_iota(jnp.int32, (rows4, 1), 0) // A_BLOCK
    lo = lax.broadcasted_iota(jnp.int32, (A_BLOCK, LANES), 1) < A_HALF
    ones = jnp.ones((2 * A_BLOCK, LANES), BF16)

    def softmax_pv(lhs, k_all, v_aug, heads, mask):
        s = lax.dot_general(lhs, k_all, (((1,), (1,)), ((), ())), preferred_element_type=F32)
        s = jnp.where(mask, s, -jnp.inf)
        sink = jnp.zeros((rows4, 1), F32)
        for n, h in enumerate(heads):
            sink = jnp.where(slab == n, sinks_ref[h], sink)
        m = jnp.maximum(jnp.max(s, axis=-1, keepdims=True), sink)
        p = jnp.exp(s - m).astype(BF16)
        o = jnp.dot(p, v_aug, preferred_element_type=F32)
        den = o[:, LANES:] + jnp.exp(sink - m)
        return o[:, :LANES] / den

    kp, kpsw, vp, vpsw = prev[0], prev[1], prev[2], prev[3]
    for j in range(A_TQ // A_BLOCK):
        rows = slice(j * A_BLOCK, (j + 1) * A_BLOCK)
        kc, kcsw, vc, vcsw = kb[rows], kswb[rows], vb[rows], vswb[rows]
        k_all = jnp.concatenate([kp, kc], axis=0)
        k_all_sw = jnp.concatenate([kpsw, kcsw], axis=0)
        v_aug = jnp.concatenate([jnp.concatenate([vp, vc], axis=0), ones], axis=1)
        v_aug_sw = jnp.concatenate([jnp.concatenate([vpsw, vcsw], axis=0), ones], axis=1)
        mask = band if j > 0 else band & ((kj >= A_BLOCK) | (step > 0))

        qj = q[rows]
        blocks = [qj[:, n * LANES:(n + 1) * LANES] for n in range(nblk)]
        zero = jnp.zeros_like(blocks[0])
        q_lo = [jnp.where(lo, x, zero) for x in blocks]
        q_hi = [jnp.where(lo, zero, x) for x in blocks]
        half = nblk // A_KV_HEADS
        lhs_a = jnp.concatenate(q_lo[:half] + q_hi[half:], axis=0)
        lhs_b = jnp.concatenate(q_hi[:half] + q_lo[half:], axis=0)
        heads_a = [2 * n for n in range(half)] + [2 * n + 1 for n in range(half, nblk)]
        heads_b = [2 * n + 1 for n in range(half)] + [2 * n for n in range(half, nblk)]
        oa = softmax_pv(lhs_a, k_all, v_aug, heads_a, mask)
        ob = softmax_pv(lhs_b, k_all_sw, v_aug_sw, heads_b, mask)
        outs = []
        for n in range(nblk):
            a_n = oa[n * A_BLOCK:(n + 1) * A_BLOCK]
            b_n = ob[n * A_BLOCK:(n + 1) * A_BLOCK]
            outs.append(jnp.where(lo, a_n, b_n) if n < half else jnp.where(lo, b_n, a_n))
        out = jnp.concatenate(outs, axis=1)
        o_ref[rows, :] = (out * _silu(az_ref[rows, :])).astype(BF16)
        kp, kpsw, vp, vpsw = kc, kcsw, vc, vcsw

    prev[0] = kp
    prev[1] = kpsw
    prev[2] = vp
    prev[3] = vpsw


def _attention(proj, sinks, cos_t, sin_t, batch, seq):
    nq = seq // A_TQ
    T = batch * seq
    assert (A_HEAD_DIM * 2 == LANES and A_KV_WIDTH == LANES
            and (A_HEADS // A_KV_HEADS) % 2 == 0 and WINDOW == A_BLOCK)
    return pl.pallas_call(
        _attn_kernel,
        grid=(batch, nq),
        in_specs=[
            pl.BlockSpec(memory_space=pltpu.SMEM),
            pl.BlockSpec((A_TQ, A_WIDTH), lambda b, i: (b * nq + i, COL_AQ // A_WIDTH)),
            pl.BlockSpec((A_TQ, 2 * A_KV_WIDTH), lambda b, i: (b * nq + i, COL_KV // (2 * A_KV_WIDTH))),
            pl.BlockSpec((A_TQ, A_WIDTH), lambda b, i: (b * nq + i, COL_AZ // A_WIDTH)),
            pl.BlockSpec((A_TQ, LANES), lambda b, i: (i, 0)),
            pl.BlockSpec((A_TQ, LANES), lambda b, i: (i, 0)),
        ],
        out_specs=pl.BlockSpec((A_TQ, A_WIDTH), lambda b, i: (b * nq + i, 0)),
        out_shape=jax.ShapeDtypeStruct((T, A_WIDTH), BF16),
        scratch_shapes=[pltpu.VMEM((4, A_BLOCK, LANES), BF16)],
        compiler_params=pltpu.CompilerParams(
            dimension_semantics=("parallel", "arbitrary"),
            vmem_limit_bytes=VMEM_LIMIT),
        name="swa_attention",
    )(sinks, proj, proj, proj, cos_t, sin_t)


R_TB = 256
SQRT_FLOOR = 1e-30


def _causal_conv(buf_ref, w, tb):
    xext = buf_ref[0:SUBLANES + tb, :]
    acc = xext[SUBLANES:, :] * w[CONV_WIDTH - 1:CONV_WIDTH, :]
    for k in range(CONV_WIDTH - 1):
        shifted = pltpu.roll(xext, CONV_WIDTH - 1 - k, 0)[SUBLANES:, :]
        acc = acc + shifted * w[k:k + 1, :]
    return acc


def _rglru_kernel(rx_ref, rz_ref, cw_ref, cb_ref, wa_ref, ba_ref, wx_ref, bx_ref, lam_ref,
                  o_ref, xbuf, acum, ucum, hcarry):
    tb = R_TB

    @pl.when(pl.program_id(1) == 0)
    def _():
        xbuf[0:SUBLANES, :] = jnp.zeros((SUBLANES, R_WIDTH), F32)
        hcarry[...] = jnp.zeros_like(hcarry)

    xbuf[SUBLANES:SUBLANES + tb, :] = rx_ref[...]
    xr = _causal_conv(xbuf, cw_ref[...], tb) + cb_ref[...]
    xbuf[0:SUBLANES, :] = xbuf[tb:tb + SUBLANES, :]

    xrb = xr.astype(BF16)
    ra, ia = [], []
    for n in range(R_BLOCKS):
        xb = xrb[:, n * R_BLOCK_DIM:(n + 1) * R_BLOCK_DIM]
        ra.append(jnp.dot(xb, wa_ref[n], preferred_element_type=F32))
        ia.append(jnp.dot(xb, wx_ref[n], preferred_element_type=F32))
    r = _sigmoid(jnp.concatenate(ra, axis=1) + ba_ref[...])
    ig = _sigmoid(jnp.concatenate(ia, axis=1) + bx_ref[...])
    log_a = (-R_C) * r * _softplus(-lam_ref[...])
    a = jnp.exp(log_a)
    s = jnp.maximum(1.0 - a * a, 0.0)
    u = (s * lax.rsqrt(jnp.maximum(s, SQRT_FLOOR))) * (ig * xr)

    row8 = lax.broadcasted_iota(jnp.int32, (tb, R_WIDTH), 0) % SUBLANES
    for k in (1, 2, 4):
        keep = row8 >= k
        a_sh = jnp.where(keep, pltpu.roll(a, k, 0), 1.0)
        u_sh = jnp.where(keep, pltpu.roll(u, k, 0), 0.0)
        u = a * u_sh + u
        a = a * a_sh
    acum[...] = a
    ucum[...] = u

    def body(g, hprev):
        r0 = pl.multiple_of(g * SUBLANES, SUBLANES)
        h = acum[pl.ds(r0, SUBLANES), :] * hprev + ucum[pl.ds(r0, SUBLANES), :]
        ucum[pl.ds(r0, SUBLANES), :] = h
        return jnp.broadcast_to(h[SUBLANES - 1:SUBLANES, :], (SUBLANES, R_WIDTH))

    hlast = lax.fori_loop(0, tb // SUBLANES, body, hcarry[...])
    hcarry[...] = hlast
    o_ref[...] = (ucum[...] * _silu(rz_ref[...])).astype(BF16)


def _rglru(proj, cw, cb, wa, ba, wx, bx, lam, batch, seq):
    tb = R_TB
    nblk = seq // tb
    T = batch * seq
    vec = lambda: pl.BlockSpec((1, R_WIDTH), lambda b, i: (0, 0))
    return pl.pallas_call(
        _rglru_kernel,
        grid=(batch, nblk),
        in_specs=[
            pl.BlockSpec((tb, R_WIDTH), lambda b, i: (b * nblk + i, COL_RX // R_WIDTH)),
            pl.BlockSpec((tb, R_WIDTH), lambda b, i: (b * nblk + i, COL_RZ // R_WIDTH)),
            pl.BlockSpec((CONV_WIDTH, R_WIDTH), lambda b, i: (0, 0)),
            vec(),
            pl.BlockSpec((R_BLOCKS, R_BLOCK_DIM, R_BLOCK_DIM), lambda b, i: (0, 0, 0)),
            vec(),
            pl.BlockSpec((R_BLOCKS, R_BLOCK_DIM, R_BLOCK_DIM), lambda b, i: (0, 0, 0)),
            vec(),
            vec(),
        ],
        out_specs=pl.BlockSpec((tb, R_WIDTH), lambda b, i: (b * nblk + i, 0)),
        out_shape=jax.ShapeDtypeStruct((T, R_WIDTH), BF16),
        scratch_shapes=[
            pltpu.VMEM((SUBLANES + tb, R_WIDTH), F32),
            pltpu.VMEM((tb, R_WIDTH), F32),
            pltpu.VMEM((tb, R_WIDTH), F32),
            pltpu.VMEM((SUBLANES, R_WIDTH), F32),
        ],
        compiler_params=pltpu.CompilerParams(
            dimension_semantics=("parallel", "arbitrary"),
            vmem_limit_bytes=VMEM_LIMIT),
        name="rglru",
    )(proj, proj, cw, cb, wa, ba, wx, bx, lam)


G_TB = 256


def _l2norm(x):
    return x * lax.rsqrt(jnp.sum(x * x, axis=-1, keepdims=True) + RMS_EPS)


def _bdot(a, b):
    return jnp.dot(a.astype(BF16), b.astype(BF16), preferred_element_type=F32)


def _unit_lower_solves(ms, rhss):
    c = ms[0].shape[0]
    eye = (lax.broadcasted_iota(jnp.int32, (c, c), 0) ==
           lax.broadcasted_iota(jnp.int32, (c, c), 1)).astype(F32)
    p_s = [eye - m for m in ms]
    a_s = [_bdot(m, m) for m in ms]
    k = 2
    while k < c // 2:
        st = [_bdot(jnp.concatenate([a, p], axis=0), a) for a, p in zip(a_s, p_s)]
        p_s = [p + s[c:] for p, s in zip(p_s, st)]
        a_s = [s[:c] for s in st]
        k *= 2
    ys = [r + _bdot(a, r) for a, r in zip(a_s, rhss)]
    return [_bdot(p, y) for p, y in zip(p_s, ys)]


G_GROUP = 4


def _gdn_kernel(gq_ref, gk_ref, gv_ref, gz_ref, sm_ref, cwq_ref, cwk_ref, cwv_ref,
                alog_ref, dtb_ref, nw_ref, o_ref,
                qbuf, kbuf, vbuf, qs, ks, vs, bg, us, ws, qds, qks, kdts, gls, state):
    tb = G_TB
    C = G_CHUNK
    nb = gq_ref.shape[0]

    @pl.when(pl.program_id(0) == 0)
    def _():
        zeros = jnp.zeros((nb, SUBLANES, G_WIDTH), F32)
        qbuf[:, 0:SUBLANES, :] = zeros
        kbuf[:, 0:SUBLANES, :] = zeros
        vbuf[:, 0:SUBLANES, :] = zeros
        state[...] = jnp.zeros_like(state)

    lane = lax.broadcasted_iota(jnp.int32, (tb, LANES), 1)
    for b in range(nb):
        for src, buf, cw, dst in ((gq_ref, qbuf, cwq_ref, qs), (gk_ref, kbuf, cwk_ref, ks),
                                  (gv_ref, vbuf, cwv_ref, vs)):
            buf[b, SUBLANES:SUBLANES + tb, :] = src[b]
            dst[b] = _silu(_causal_conv(buf.at[b], cw[...], tb))
            buf[b, 0:SUBLANES, :] = buf[b, tb:tb + SUBLANES, :]
        small = sm_ref[b]
        g_all = -jnp.exp(alog_ref[...]) * _softplus(small + dtb_ref[...])
        bg[b] = jnp.where(lane < G_HEADS, _sigmoid(small), g_all)

    ri = lax.broadcasted_iota(jnp.int32, (C, C), 0)
    ci = lax.broadcasted_iota(jnp.int32, (C, C), 1)
    tril = ri >= ci
    strict = ri > ci
    tril_f = tril.astype(F32)

    def intra(grp, carry):
        dests, ms, rhss = [], [], []
        for b in range(nb):
            for cc in range(G_GROUP):
                c = grp * G_GROUP + cc
                r0 = pl.multiple_of(c * C, C)
                bgc = bg[b, pl.ds(r0, C), :]
                gcum_col = jnp.dot(tril_f, bgc, precision=HIGHEST, preferred_element_type=F32)
                gcum_row = gcum_col.T[:SUBLANES, :]
                for h in range(G_HEADS):
                    cols = slice(h * G_HEAD_DIM, (h + 1) * G_HEAD_DIM)
                    qn = _l2norm(qs[b, pl.ds(r0, C), cols]) * (G_HEAD_DIM ** -0.5)
                    kn = _l2norm(ks[b, pl.ds(r0, C), cols])
                    v = vs[b, pl.ds(r0, C), cols]
                    beta = jnp.broadcast_to(bgc[:, h:h + 1], (C, G_HEAD_DIM))
                    gc = jnp.broadcast_to(gcum_col[:, G_HEADS + h:G_HEADS + h + 1], (C, G_HEAD_DIM))
                    gr = gcum_row[G_HEADS + h:G_HEADS + h + 1, :]
                    g_last = gc[C - 1:C, :]
                    decay = jnp.where(tril, jnp.exp(gc[:, :C] - gr), 0.0)
                    eg = jnp.exp(gc)
                    knb = kn.astype(BF16)
                    qk_kk = lax.dot_general(jnp.concatenate([qn.astype(BF16), knb], axis=0), knb,
                                            (((1,), (1,)), ((), ())), preferred_element_type=F32)
                    qks[b, pl.ds(r0, C), h * C:(h + 1) * C] = (qk_kk[:C] * decay).astype(BF16)
                    ms.append(jnp.where(strict, qk_kk[C:] * beta[:, :C] * decay, 0.0))
                    qds[b, pl.ds(r0, C), cols] = (qn * eg).astype(BF16)
                    kdts[b, c * G_HEADS + h] = (kn * jnp.exp(g_last - gc)).T.astype(BF16)
                    gls[b, c * G_HEADS + h] = jnp.broadcast_to(jnp.exp(g_last), (SUBLANES, LANES))
                    rhss.append(jnp.concatenate([v * beta, kn * (beta * eg)], axis=1))
                    dests.append((b, r0, cols))
        for (b, r0, cols), uw in zip(dests, _unit_lower_solves(ms, rhss)):
            us[b, pl.ds(r0, C), cols] = uw[:, :G_HEAD_DIM]
            ws[b, pl.ds(r0, C), cols] = uw[:, G_HEAD_DIM:].astype(BF16)
        return carry

    lax.fori_loop(0, tb // (C * G_GROUP), intra, 0)

    chains = [(b, h) for b in range(nb) for h in range(G_HEADS)]

    def inter(c, carry):
        r0 = pl.multiple_of(c * C, C)
        rows = pl.ds(r0, C)
        hcols = lambda h: slice(h * G_HEAD_DIM, (h + 1) * G_HEAD_DIM)
        s_old = [state[b, h] for b, h in chains]
        sb = [s.astype(BF16) for s in s_old]
        w_s = [jnp.dot(ws[b, rows, hcols(h)], s, preferred_element_type=F32)
               for (b, h), s in zip(chains, sb)]
        vnb = [(us[b, rows, hcols(h)] - x).astype(BF16) for (b, h), x in zip(chains, w_s)]
        upd = [jnp.dot(kdts[b, c * G_HEADS + h], v, preferred_element_type=F32)
               for (b, h), v in zip(chains, vnb)]
        for (b, h), s, x in zip(chains, s_old, upd):
            state[b, h] = s * gls[b, c * G_HEADS + h][0:1, :] + x
        for (b, h), s, v in zip(chains, sb, vnb):
            o = (jnp.dot(qds[b, rows, hcols(h)], s, preferred_element_type=F32)
                 + jnp.dot(qks[b, rows, h * C:(h + 1) * C], v, preferred_element_type=F32))
            on = o * lax.rsqrt(jnp.mean(o * o, axis=-1, keepdims=True) + RMS_EPS) * nw_ref[...]
            o_ref[b, rows, hcols(h)] = (on * _silu(gz_ref[b, rows, hcols(h)])).astype(BF16)
        return carry

    lax.fori_loop(0, tb // C, inter, 0, unroll=True)


def _gdn(proj, cwq, cwk, cwv, alog, dtb, nw, batch, seq):
    tb = G_TB
    nchunk = tb // G_CHUNK
    proj3 = proj.reshape(batch, seq, N_PROJ)
    seg = lambda col: pl.BlockSpec((batch, tb, G_WIDTH), lambda i: (0, i, col // G_WIDTH))
    cw = lambda: pl.BlockSpec((CONV_WIDTH, G_WIDTH), lambda i: (0, 0))
    vec = lambda: pl.BlockSpec((1, LANES), lambda i: (0, 0))
    out = pl.pallas_call(
        _gdn_kernel,
        grid=(seq // tb,),
        in_specs=[seg(COL_GQ), seg(COL_GK), seg(COL_GV), seg(COL_GZ),
                  pl.BlockSpec((batch, tb, LANES), lambda i: (0, i, COL_SMALL // LANES)),
                  cw(), cw(), cw(), vec(), vec(), vec()],
        out_specs=pl.BlockSpec((batch, tb, G_WIDTH), lambda i: (0, i, 0)),
        out_shape=jax.ShapeDtypeStruct((batch, seq, G_WIDTH), BF16),
        scratch_shapes=[
            pltpu.VMEM((batch, SUBLANES + tb, G_WIDTH), F32),
            pltpu.VMEM((batch, SUBLANES + tb, G_WIDTH), F32),
            pltpu.VMEM((batch, SUBLANES + tb, G_WIDTH), F32),
            pltpu.VMEM((batch, tb, G_WIDTH), F32),
            pltpu.VMEM((batch, tb, G_WIDTH), F32),
            pltpu.VMEM((batch, tb, G_WIDTH), F32),
            pltpu.VMEM((batch, tb, LANES), F32),
            pltpu.VMEM((batch, tb, G_WIDTH), F32),
            pltpu.VMEM((batch, tb, G_WIDTH), BF16),
            pltpu.VMEM((batch, tb, G_WIDTH), BF16),
            pltpu.VMEM((batch, tb, G_HEADS * G_CHUNK), BF16),
            pltpu.VMEM((batch, nchunk * G_HEADS, G_HEAD_DIM, G_CHUNK), BF16),
            pltpu.VMEM((batch, nchunk * G_HEADS, SUBLANES, LANES), F32),
            pltpu.VMEM((batch, G_HEADS, G_HEAD_DIM, G_HEAD_DIM), F32),
        ],
        compiler_params=pltpu.CompilerParams(
            dimension_semantics=("arbitrary",),
            vmem_limit_bytes=VMEM_LIMIT),
        name="gated_deltanet",
    )(proj3, proj3, proj3, proj3, proj3, cwq, cwk, cwv, alog, dtb, nw)
    return out.reshape(batch * seq, G_WIDTH)


OUT_TM = 512


def _outproj_kernel(ya_ref, yr_ref, yg_ref, x_ref, w_ref, g_ref, b_ref, o_ref, ob_ref):
    half = OUT_TM // 2
    for r in range(2):
        rows = slice(r * half, (r + 1) * half)
        y = jnp.dot(ya_ref[rows, :], w_ref[0:A_WIDTH, :], preferred_element_type=F32)
        y += jnp.dot(yr_ref[rows, :], w_ref[A_WIDTH:A_WIDTH + R_WIDTH, :], preferred_element_type=F32)
        y += jnp.dot(yg_ref[rows, :], w_ref[A_WIDTH + R_WIDTH:MIX_WIDTH, :], preferred_element_type=F32)
        z = DEEPNORM_ALPHA * x_ref[rows, :] + y
        mu = jnp.mean(z, axis=-1, keepdims=True)
        zc = z - mu
        var = jnp.mean(zc * zc, axis=-1, keepdims=True)
        out = zc * lax.rsqrt(var + LN_EPS) * g_ref[...] + b_ref[...]
        o_ref[rows, :] = out
        ob_ref[rows, :] = out.astype(BF16)


def _outproj(ya, yr, yg, x, w, layer, g, b):
    T = x.shape[0]
    tm = OUT_TM
    row = lambda width: pl.BlockSpec((tm, width), lambda i: (i, 0))
    vec = lambda: pl.BlockSpec((1, D_MODEL), lambda i: (0, 0))
    return pl.pallas_call(
        _outproj_kernel,
        grid=(T // tm,),
        in_specs=[row(A_WIDTH), row(R_WIDTH), row(G_WIDTH), row(D_MODEL),
                  pl.BlockSpec((None, MIX_WIDTH, D_MODEL), lambda i: (layer, 0, 0),
                               pipeline_mode=pl.Buffered(1)),
                  vec(), vec()],
        out_specs=[row(D_MODEL), row(D_MODEL)],
        out_shape=[jax.ShapeDtypeStruct((T, D_MODEL), F32),
                   jax.ShapeDtypeStruct((T, D_MODEL), BF16)],
        compiler_params=pltpu.CompilerParams(
            dimension_semantics=("parallel",),
            vmem_limit_bytes=VMEM_LIMIT),
        name="outproj_deepnorm",
    )(ya, yr, yg, x, w, g, b)


def _rope_tables(seq):
    half = A_HEAD_DIM // 2
    inv = 1.0 / (ROPE_THETA ** (jnp.arange(0, A_HEAD_DIM, 2, dtype=F32) / A_HEAD_DIM))
    ang = jnp.arange(seq, dtype=F32)[:, None] * inv[None, :]
    cos, sin = jnp.cos(ang), jnp.sin(ang)
    reps = LANES // A_HEAD_DIM
    cos_t = jnp.tile(jnp.concatenate([cos, cos], axis=1), (1, reps))
    sin_t = jnp.tile(jnp.concatenate([-sin, sin], axis=1), (1, reps))
    assert cos_t.shape == (seq, LANES) and half * 2 == A_HEAD_DIM
    return cos_t, sin_t


def _lane_vec(v, offset):
    return jnp.zeros((1, LANES), F32).at[0, offset:offset + v.shape[0]].set(v.astype(F32))


@jax.jit
def _forward(x, w_in, sinks, r_conv_w, r_conv_b, r_wa, r_ba, r_wx, r_bx, r_lam,
             g_conv_w, g_a_log, g_dt_bias, g_norm_w, w_out, ln_g, ln_b):
    batch, seq, _ = x.shape
    T = batch * seq
    cos_t, sin_t = _rope_tables(seq)
    w_out_b = w_out.astype(BF16)
    w_in_t = jnp.swapaxes(w_in, 1, 2)
    xf = x.reshape(T, D_MODEL)
    xb = xf.astype(BF16)
    for l in range(DEPTH):
        proj = _inproj(xb, w_in_t, l)
        ya = _attention(proj, sinks[l], cos_t, sin_t, batch, seq)
        yr = _rglru(proj, r_conv_w[l], r_conv_b[l][None, :], r_wa[l].astype(BF16), r_ba[l][None, :],
                    r_wx[l].astype(BF16), r_bx[l][None, :], r_lam[l][None, :], batch, seq)
        gcw = g_conv_w[l]
        yg = _gdn(proj, gcw[:, :G_WIDTH], gcw[:, G_WIDTH:2 * G_WIDTH], gcw[:, 2 * G_WIDTH:],
                  _lane_vec(g_a_log[l], G_HEADS), _lane_vec(g_dt_bias[l], G_HEADS),
                  g_norm_w[l][None, :], batch, seq)
        xf, xb = _outproj(ya, yr, yg, xf, w_out_b, l, ln_g[l][None, :], ln_b[l][None, :])
    return xf.reshape(batch, seq, D_MODEL)


def kernel(x, w_in, sinks, r_conv_w, r_conv_b, r_wa, r_ba, r_wx, r_bx, r_lam, g_conv_w, g_a_log,
           g_dt_bias, g_norm_w, w_out, ln_g, ln_b):
    return _forward(x, w_in, sinks, r_conv_w, r_conv_b, r_wa, r_ba, r_wx, r_bx, r_lam,
                    g_conv_w, g_a_log, g_dt_bias, g_norm_w, w_out, ln_g, ln_b)
```

```python
import functools
import math

import numpy as np
import jax
import jax.numpy as jnp
from jax import lax
from jax.experimental import pallas as pl
from jax.experimental.pallas import tpu as pltpu

D_MODEL = 2048
DEPTH = 2
A_HEADS = 8
A_KV_HEADS = 2
A_HEAD_DIM = 64
A_WIDTH = A_HEADS * A_HEAD_DIM
A_KV_WIDTH = A_KV_HEADS * A_HEAD_DIM
WINDOW = 128
A_BLOCK = 128
ROPE_THETA = 10000.0
R_WIDTH = 1024
R_BLOCKS = 8
R_BLOCK_DIM = R_WIDTH // R_BLOCKS
R_C = 8.0
CONV_WIDTH = 4
G_HEADS = 4
G_HEAD_DIM = 128
G_WIDTH = G_HEADS * G_HEAD_DIM
G_CHUNK = 64
MIX_WIDTH = A_WIDTH + R_WIDTH + G_WIDTH
DEEPNORM_ALPHA = (2 * DEPTH) ** 0.25
LN_EPS = 1e-5
RMS_EPS = 1e-6

LANES = 128
SUBLANES = 8
VMEM_LIMIT = 56 * 1024 * 1024

COL_RX = 0
COL_RZ = 1024
COL_AQ = 2048
COL_AZ = 2560
COL_GQ = 3072
COL_GK = 3584
COL_GV = 4096
COL_GZ = 4608
COL_KV = 5120
COL_SMALL = 5376
N_PROJ = 5632

F32 = jnp.float32
BF16 = jnp.bfloat16
HIGHEST = lax.Precision.HIGHEST


def _sigmoid(x):
    return 0.5 * jnp.tanh(0.5 * x) + 0.5


def _silu(x):
    return x * _sigmoid(x)


def _softplus(x):
    return jnp.maximum(x, 0.0) + jnp.log1p(jnp.exp(-jnp.abs(x)))


N_IN = 5384
IN_TM = 2048
IN_TN = 512
IN_SUB = 256
IN_TILE_ORDER = (5, 6, 7, 8, 9, 10, 11, 12, 0, 1, 3, 4, 13, 14, 15, 16, 17, 18, 19, 20, 2, 21)


def _inproj_kernel(order_ref, x_ref, w0_ref, w1_ref, o_ref, wb):
    j = pl.program_id(0)

    @pl.when(pl.program_id(1) == 0)
    def _():
        row = lax.broadcasted_iota(jnp.int32, (IN_SUB, D_MODEL), 0)
        for t, w_ref in enumerate((w0_ref, w1_ref)):
            valid = N_IN - order_ref[2 * j + t] * IN_SUB
            wb[t * IN_SUB:(t + 1) * IN_SUB, :] = jnp.where(row < valid, w_ref[...], 0.0).astype(BF16)

    o_ref[...] = lax.dot_general(x_ref[...], wb[...], (((1,), (1,)), ((), ())),
                                 preferred_element_type=F32)


def _inproj(xb, w_in_t, layer):
    T = xb.shape[0]
    per_step = IN_TN // IN_SUB
    assert per_step == 2 and len(IN_TILE_ORDER) * IN_SUB == N_PROJ
    order = jnp.asarray(IN_TILE_ORDER, jnp.int32)
    w_spec = lambda t: pl.BlockSpec((None, IN_SUB, D_MODEL),
                                    lambda j, i, order_ref: (layer, order_ref[per_step * j + t], 0))
    return pl.pallas_call(
        _inproj_kernel,
        grid_spec=pltpu.PrefetchScalarGridSpec(
            num_scalar_prefetch=1,
            grid=(N_PROJ // IN_TN, T // IN_TM),
            in_specs=[pl.BlockSpec((IN_TM, D_MODEL), lambda j, i, order_ref: (i, 0)),
                      w_spec(0), w_spec(1)],
            out_specs=pl.BlockSpec((IN_TM, IN_TN), lambda j, i, order_ref: (i, j)),
            scratch_shapes=[pltpu.VMEM((IN_TN, D_MODEL), BF16)]),
        out_shape=jax.ShapeDtypeStruct((T, N_PROJ), F32),
        compiler_params=pltpu.CompilerParams(
            dimension_semantics=("arbitrary", "arbitrary"),
            vmem_limit_bytes=VMEM_LIMIT),
        name="inproj",
    )(order, xb, w_in_t, w_in_t)


def _rope(x, cos, sin_signed):
    w = x.shape[1]
    reps = w // LANES
    if reps > 1:
        cos = jnp.concatenate([cos] * reps, axis=1)
        sin_signed = jnp.concatenate([sin_signed] * reps, axis=1)
    lane = lax.broadcasted_iota(jnp.int32, x.shape, 1)
    first_half = (lane % A_HEAD_DIM) < (A_HEAD_DIM // 2)
    swapped = jnp.where(first_half,
                        pltpu.roll(x, w - A_HEAD_DIM // 2, 1),
                        pltpu.roll(x, A_HEAD_DIM // 2, 1))
    return x * cos + swapped * sin_signed


A_TQ = 128
A_HALF = LANES // 2


def _attn_kernel(sinks_ref, q_ref, kv_ref, az_ref, cos_ref, sin_ref, o_ref, prev):
    step = pl.program_id(1)
    nblk = A_WIDTH // LANES

    @pl.when(step == 0)
    def _():
        prev[...] = jnp.zeros_like(prev)

    cos = cos_ref[...]
    sin = sin_ref[...]
    q = (_rope(q_ref[...], cos, sin) * (A_HEAD_DIM ** -0.5)).astype(BF16)
    kv = kv_ref[...]
    k = _rope(kv[:, :A_KV_WIDTH], cos, sin)
    v = kv[:, A_KV_WIDTH:]
    kb = k.astype(BF16)
    vb = v.astype(BF16)
    kswb = pltpu.roll(k, A_HALF, 1).astype(BF16)
    vswb = pltpu.roll(v, A_HALF, 1).astype(BF16)

    rows4 = nblk * A_BLOCK
    qi = lax.broadcasted_iota(jnp.int32, (rows4, 2 * A_BLOCK), 0) % A_BLOCK
    kj = lax.broadcasted_iota(jnp.int32, (rows4, 2 * A_BLOCK), 1)
    diff = qi - kj + A_BLOCK
    band = (diff >= 0) & (diff < WINDOW)
    slab = lax.broadcasted_iota(jnp.int32, (rows4, 1), 0) // A_BLOCK
    lo = lax.broadcasted_iota(jnp.int32, (A_BLOCK, LANES), 1) < A_HALF
    ones = jnp.ones((2 * A_BLOCK, LANES), BF16)

    def softmax_pv(lhs, k_all, v_aug, heads, mask):
        s = lax.dot_general(lhs, k_all, (((1,), (1,)), ((), ())), preferred_element_type=F32)
        s = jnp.where(mask, s, -jnp.inf)
        sink = jnp.zeros((rows4, 1), F32)
        for n, h in enumerate(heads):
            sink = jnp.where(slab == n, sinks_ref[h], sink)
        m = jnp.maximum(jnp.max(s, axis=-1, keepdims=True), sink)
        p = jnp.exp(s - m).astype(BF16)
        o = jnp.dot(p, v_aug, preferred_element_type=F32)
        den = o[:, LANES:] + jnp.exp(sink - m)
        return o[:, :LANES] / den

    kp, kpsw, vp, vpsw = prev[0], prev[1], prev[2], prev[3]
    for j in range(A_TQ // A_BLOCK):
        rows = slice(j * A_BLOCK, (j + 1) * A_BLOCK)
        kc, kcsw, vc, vcsw = kb[rows], kswb[rows], vb[rows], vswb[rows]
        k_all = jnp.concatenate([kp, kc], axis=0)
        k_all_sw = jnp.concatenate([kpsw, kcsw], axis=0)
        v_aug = jnp.concatenate([jnp.concatenate([vp, vc], axis=0), ones], axis=1)
        v_aug_sw = jnp.concatenate([jnp.concatenate([vpsw, vcsw], axis=0), ones], axis=1)
        mask = band if j > 0 else band & ((kj >= A_BLOCK) | (step > 0))

        qj = q[rows]
        blocks = [qj[:, n * LANES:(n + 1) * LANES] for n in range(nblk)]
        zero = jnp.zeros_like(blocks[0])
        q_lo = [jnp.where(lo, x, zero) for x in blocks]
        q_hi = [jnp.where(lo, zero, x) for x in blocks]
        half = nblk // A_KV_HEADS
        lhs_a = jnp.concatenate(q_lo[:half] + q_hi[half:], axis=0)
        lhs_b = jnp.concatenate(q_hi[:half] + q_lo[half:], axis=0)
        heads_a = [2 * n for n in range(half)] + [2 * n + 1 for n in range(half, nblk)]
        heads_b = [2 * n + 1 for n in range(half)] + [2 * n for n in range(half, nblk)]
        oa = softmax_pv(lhs_a, k_all, v_aug, heads_a, mask)
        ob = softmax_pv(lhs_b, k_all_sw, v_aug_sw, heads_b, mask)
        outs = []
        for n in range(nblk):
            a_n = oa[n * A_BLOCK:(n + 1) * A_BLOCK]
            b_n = ob[n * A_BLOCK:(n + 1) * A_BLOCK]
            outs.append(jnp.where(lo, a_n, b_n) if n < half else jnp.where(lo, b_n, a_n))
        out = jnp.concatenate(outs, axis=1)
        o_ref[rows, :] = (out * _silu(az_ref[rows, :])).astype(BF16)
        kp, kpsw, vp, vpsw = kc, kcsw, vc, vcsw

    prev[0] = kp
    prev[1] = kpsw
    prev[2] = vp
    prev[3] = vpsw


def _attention(proj, sinks, cos_t, sin_t, batch, seq):
    nq = seq // A_TQ
    T = batch * seq
    assert (A_HEAD_DIM * 2 == LANES and A_KV_WIDTH == LANES
            and (A_HEADS // A_KV_HEADS) % 2 == 0 and WINDOW == A_BLOCK)
    return pl.pallas_call(
        _attn_kernel,
        grid=(batch, nq),
        in_specs=[
            pl.BlockSpec(memory_space=pltpu.SMEM),
            pl.BlockSpec((A_TQ, A_WIDTH), lambda b, i: (b * nq + i, COL_AQ // A_WIDTH)),
            pl.BlockSpec((A_TQ, 2 * A_KV_WIDTH), lambda b, i: (b * nq + i, COL_KV // (2 * A_KV_WIDTH))),
            pl.BlockSpec((A_TQ, A_WIDTH), lambda b, i: (b * nq + i, COL_AZ // A_WIDTH)),
            pl.BlockSpec((A_TQ, LANES), lambda b, i: (i, 0)),
            pl.BlockSpec((A_TQ, LANES), lambda b, i: (i, 0)),
        ],
        out_specs=pl.BlockSpec((A_TQ, A_WIDTH), lambda b, i: (b * nq + i, 0)),
        out_shape=jax.ShapeDtypeStruct((T, A_WIDTH), BF16),
        scratch_shapes=[pltpu.VMEM((4, A_BLOCK, LANES), BF16)],
        compiler_params=pltpu.CompilerParams(
            dimension_semantics=("parallel", "arbitrary"),
            vmem_limit_bytes=VMEM_LIMIT),
        name="swa_attention",
    )(sinks, proj, proj, proj, cos_t, sin_t)


R_TB = 256
SQRT_FLOOR = 1e-30


def _causal_conv(buf_ref, w, tb):
    xext = buf_ref[0:SUBLANES + tb, :]
    acc = xext[SUBLANES:, :] * w[CONV_WIDTH - 1:CONV_WIDTH, :]
    for k in range(CONV_WIDTH - 1):
        shifted = pltpu.roll(xext, CONV_WIDTH - 1 - k, 0)[SUBLANES:, :]
        acc = acc + shifted * w[k:k + 1, :]
    return acc


def _rglru_kernel(rx_ref, rz_ref, cw_ref, cb_ref, wa_ref, ba_ref, wx_ref, bx_ref, lam_ref,
                  o_ref, xbuf, acum, ucum, hcarry):
    tb = R_TB

    @pl.when(pl.program_id(1) == 0)
    def _():
        xbuf[0:SUBLANES, :] = jnp.zeros((SUBLANES, R_WIDTH), F32)
        hcarry[...] = jnp.zeros_like(hcarry)

    xbuf[SUBLANES:SUBLANES + tb, :] = rx_ref[...]
    xr = _causal_conv(xbuf, cw_ref[...], tb) + cb_ref[...]
    xbuf[0:SUBLANES, :] = xbuf[tb:tb + SUBLANES, :]

    xrb = xr.astype(BF16)
    ra, ia = [], []
    for n in range(R_BLOCKS):
        xb = xrb[:, n * R_BLOCK_DIM:(n + 1) * R_BLOCK_DIM]
        ra.append(jnp.dot(xb, wa_ref[n], preferred_element_type=F32))
        ia.append(jnp.dot(xb, wx_ref[n], preferred_element_type=F32))
    r = _sigmoid(jnp.concatenate(ra, axis=1) + ba_ref[...])
    ig = _sigmoid(jnp.concatenate(ia, axis=1) + bx_ref[...])
    log_a = (-R_C) * r * _softplus(-lam_ref[...])
    a = jnp.exp(log_a)
    s = jnp.maximum(1.0 - a * a, 0.0)
    u = (s * lax.rsqrt(jnp.maximum(s, SQRT_FLOOR))) * (ig * xr)

    row8 = lax.broadcasted_iota(jnp.int32, (tb, R_WIDTH), 0) % SUBLANES
    for k in (1, 2, 4):
        keep = row8 >= k
        a_sh = jnp.where(keep, pltpu.roll(a, k, 0), 1.0)
        u_sh = jnp.where(keep, pltpu.roll(u, k, 0), 0.0)
        u = a * u_sh + u
        a = a * a_sh
    acum[...] = a
    ucum[...] = u

    def body(g, hprev):
        r0 = pl.multiple_of(g * SUBLANES, SUBLANES)
        h = acum[pl.ds(r0, SUBLANES), :] * hprev + ucum[pl.ds(r0, SUBLANES), :]
        ucum[pl.ds(r0, SUBLANES), :] = h
        return jnp.broadcast_to(h[SUBLANES - 1:SUBLANES, :], (SUBLANES, R_WIDTH))

    hlast = lax.fori_loop(0, tb // SUBLANES, body, hcarry[...])
    hcarry[...] = hlast
    o_ref[...] = (ucum[...] * _silu(rz_ref[...])).astype(BF16)


def _rglru(proj, cw, cb, wa, ba, wx, bx, lam, batch, seq):
    tb = R_TB
    nblk = seq // tb
    T = batch * seq
    vec = lambda: pl.BlockSpec((1, R_WIDTH), lambda b, i: (0, 0))
    return pl.pallas_call(
        _rglru_kernel,
        grid=(batch, nblk),
        in_specs=[
            pl.BlockSpec((tb, R_WIDTH), lambda b, i: (b * nblk + i, COL_RX // R_WIDTH)),
            pl.BlockSpec((tb, R_WIDTH), lambda b, i: (b * nblk + i, COL_RZ // R_WIDTH)),
            pl.BlockSpec((CONV_WIDTH, R_WIDTH), lambda b, i: (0, 0)),
            vec(),
            pl.BlockSpec((R_BLOCKS, R_BLOCK_DIM, R_BLOCK_DIM), lambda b, i: (0, 0, 0)),
            vec(),
            pl.BlockSpec((R_BLOCKS, R_BLOCK_DIM, R_BLOCK_DIM), lambda b, i: (0, 0, 0)),
            vec(),
            vec(),
        ],
        out_specs=pl.BlockSpec((tb, R_WIDTH), lambda b, i: (b * nblk + i, 0)),
        out_shape=jax.ShapeDtypeStruct((T, R_WIDTH), BF16),
        scratch_shapes=[
            pltpu.VMEM((SUBLANES + tb, R_WIDTH), F32),
            pltpu.VMEM((tb, R_WIDTH), F32),
            pltpu.VMEM((tb, R_WIDTH), F32),
            pltpu.VMEM((SUBLANES, R_WIDTH), F32),
        ],
        compiler_params=pltpu.CompilerParams(
            dimension_semantics=("parallel", "arbitrary"),
            vmem_limit_bytes=VMEM_LIMIT),
        name="rglru",
    )(proj, proj, cw, cb, wa, ba, wx, bx, lam)


G_TB = 256


def _l2norm(x):
    return x * lax.rsqrt(jnp.sum(x * x, axis=-1, keepdims=True) + RMS_EPS)


def _bdot(a, b):
    return jnp.dot(a.astype(BF16), b.astype(BF16), preferred_element_type=F32)


def _unit_lower_solves(ms, rhss):
    c = ms[0].shape[0]
    eye = (lax.broadcasted_iota(jnp.int32, (c, c), 0) ==
           lax.broadcasted_iota(jnp.int32, (c, c), 1)).astype(F32)
    p_s = [eye - m for m in ms]
    a_s = [_bdot(m, m) for m in ms]
    k = 2
    while k < c // 2:
        st = [_bdot(jnp.concatenate([a, p], axis=0), a) for a, p in zip(a_s, p_s)]
        p_s = [p + s[c:] for p, s in zip(p_s, st)]
        a_s = [s[:c] for s in st]
        k *= 2
    ys = [r + _bdot(a, r) for a, r in zip(a_s, rhss)]
    return [_bdot(p, y) for p, y in zip(p_s, ys)]


G_GROUP = 2


def _gdn_kernel(gq_ref, gk_ref, gv_ref, gz_ref, sm_ref, cwq_ref, cwk_ref, cwv_ref,
                alog_ref, dtb_ref, nw_ref, o_ref,
                qbuf, kbuf, vbuf, qs, ks, vs, bg, us, ws, qds, qks, kdts, gls, state):
    tb = G_TB
    C = G_CHUNK
    nb = gq_ref.shape[0]

    @pl.when(pl.program_id(0) == 0)
    def _():
        zeros = jnp.zeros((nb, SUBLANES, G_WIDTH), F32)
        qbuf[:, 0:SUBLANES, :] = zeros
        kbuf[:, 0:SUBLANES, :] = zeros
        vbuf[:, 0:SUBLANES, :] = zeros
        state[...] = jnp.zeros_like(state)

    lane = lax.broadcasted_iota(jnp.int32, (tb, LANES), 1)
    for b in range(nb):
        for src, buf, cw, dst in ((gq_ref, qbuf, cwq_ref, qs), (gk_ref, kbuf, cwk_ref, ks),
                                  (gv_ref, vbuf, cwv_ref, vs)):
            buf[b, SUBLANES:SUBLANES + tb, :] = src[b]
            dst[b] = _silu(_causal_conv(buf.at[b], cw[...], tb))
            buf[b, 0:SUBLANES, :] = buf[b, tb:tb + SUBLANES, :]
        small = sm_ref[b]
        g_all = -jnp.exp(alog_ref[...]) * _softplus(small + dtb_ref[...])
        bg[b] = jnp.where(lane < G_HEADS, _sigmoid(small), g_all)

    ri = lax.broadcasted_iota(jnp.int32, (C, C), 0)
    ci = lax.broadcasted_iota(jnp.int32, (C, C), 1)
    tril = ri >= ci
    strict = ri > ci
    tril_f = tril.astype(F32)

    def intra(grp, carry):
        dests, ms, rhss = [], [], []
        for b in range(nb):
            for cc in range(G_GROUP):
                c = grp * G_GROUP + cc
                r0 = pl.multiple_of(c * C, C)
                bgc = bg[b, pl.ds(r0, C), :]
                gcum_col = jnp.dot(tril_f, bgc, precision=HIGHEST, preferred_element_type=F32)
                gcum_row = gcum_col.T[:SUBLANES, :]
                for h in range(G_HEADS):
                    cols = slice(h * G_HEAD_DIM, (h + 1) * G_HEAD_DIM)
                    qn = _l2norm(qs[b, pl.ds(r0, C), cols]) * (G_HEAD_DIM ** -0.5)
                    kn = _l2norm(ks[b, pl.ds(r0, C), cols])
                    v = vs[b, pl.ds(r0, C), cols]
                    beta = jnp.broadcast_to(bgc[:, h:h + 1], (C, G_HEAD_DIM))
                    gc = jnp.broadcast_to(gcum_col[:, G_HEADS + h:G_HEADS + h + 1], (C, G_HEAD_DIM))
                    gr = gcum_row[G_HEADS + h:G_HEADS + h + 1, :]
                    g_last = gc[C - 1:C, :]
                    decay = jnp.where(tril, jnp.exp(gc[:, :C] - gr), 0.0)
                    eg = jnp.exp(gc)
                    knb = kn.astype(BF16)
                    qk_kk = lax.dot_general(jnp.concatenate([qn.astype(BF16), knb], axis=0), knb,
                                            (((1,), (1,)), ((), ())), preferred_element_type=F32)
                    qks[b, pl.ds(r0, C), h * C:(h + 1) * C] = (qk_kk[:C] * decay).astype(BF16)
                    ms.append(jnp.where(strict, qk_kk[C:] * beta[:, :C] * decay, 0.0))
                    qds[b, pl.ds(r0, C), cols] = (qn * eg).astype(BF16)
                    kdts[b, c * G_HEADS + h] = (kn * jnp.exp(g_last - gc)).T.astype(BF16)
                    gls[b, c * G_HEADS + h] = jnp.broadcast_to(jnp.exp(g_last), (SUBLANES, LANES))
                    rhss.append(jnp.concatenate([v * beta, kn * (beta * eg)], axis=1))
                    dests.append((b, r0, cols))
        for (b, r0, cols), uw in zip(dests, _unit_lower_solves(ms, rhss)):
            us[b, pl.ds(r0, C), cols] = uw[:, :G_HEAD_DIM]
            ws[b, pl.ds(r0, C), cols] = uw[:, G_HEAD_DIM:].astype(BF16)
        return carry

    for grp in range(tb // (C * G_GROUP)):
        intra(grp, 0)

    chains = [(b, h) for b in range(nb) for h in range(G_HEADS)]

    def inter(c, carry):
        r0 = pl.multiple_of(c * C, C)
        rows = pl.ds(r0, C)
        hcols = lambda h: slice(h * G_HEAD_DIM, (h + 1) * G_HEAD_DIM)
        s_old = [state[b, h] for b, h in chains]
        sb = [s.astype(BF16) for s in s_old]
        w_s = [jnp.dot(ws[b, rows, hcols(h)], s, preferred_element_type=F32)
               for (b, h), s in zip(chains, sb)]
        vnb = [(us[b, rows, hcols(h)] - x).astype(BF16) for (b, h), x in zip(chains, w_s)]
        upd = [jnp.dot(kdts[b, c * G_HEADS + h], v, preferred_element_type=F32)
               for (b, h), v in zip(chains, vnb)]
        for (b, h), s, x in zip(chains, s_old, upd):
            state[b, h] = s * gls[b, c * G_HEADS + h][0:1, :] + x
        for (b, h), s, v in zip(chains, sb, vnb):
            o = (jnp.dot(qds[b, rows, hcols(h)], s, preferred_element_type=F32)
                 + jnp.dot(qks[b, rows, h * C:(h + 1) * C], v, preferred_element_type=F32))
            on = o * lax.rsqrt(jnp.mean(o * o, axis=-1, keepdims=True) + RMS_EPS) * nw_ref[...]
            o_ref[b, rows, hcols(h)] = (on * _silu(gz_ref[b, rows, hcols(h)])).astype(BF16)
        return carry

    lax.fori_loop(0, tb // C, inter, 0, unroll=True)


def _gdn(proj, cwq, cwk, cwv, alog, dtb, nw, batch, seq):
    tb = G_TB
    nchunk = tb // G_CHUNK
    proj3 = proj.reshape(batch, seq, N_PROJ)
    seg = lambda col: pl.BlockSpec((batch, tb, G_WIDTH), lambda i: (0, i, col // G_WIDTH))
    cw = lambda: pl.BlockSpec((CONV_WIDTH, G_WIDTH), lambda i: (0, 0))
    vec = lambda: pl.BlockSpec((1, LANES), lambda i: (0, 0))
    out = pl.pallas_call(
        _gdn_kernel,
        grid=(seq // tb,),
        in_specs=[seg(COL_GQ), seg(COL_GK), seg(COL_GV), seg(COL_GZ),
                  pl.BlockSpec((batch, tb, LANES), lambda i: (0, i, COL_SMALL // LANES)),
                  cw(), cw(), cw(), vec(), vec(), vec()],
        out_specs=pl.BlockSpec((batch, tb, G_WIDTH), lambda i: (0, i, 0)),
        out_shape=jax.ShapeDtypeStruct((batch, seq, G_WIDTH), BF16),
        scratch_shapes=[
            pltpu.VMEM((batch, SUBLANES + tb, G_WIDTH), F32),
            pltpu.VMEM((batch, SUBLANES + tb, G_WIDTH), F32),
            pltpu.VMEM((batch, SUBLANES + tb, G_WIDTH), F32),
            pltpu.VMEM((batch, tb, G_WIDTH), F32),
            pltpu.VMEM((batch, tb, G_WIDTH), F32),
            pltpu.VMEM((batch, tb, G_WIDTH), F32),
            pltpu.VMEM((batch, tb, LANES), F32),
            pltpu.VMEM((batch, tb, G_WIDTH), F32),
            pltpu.VMEM((batch, tb, G_WIDTH), BF16),
            pltpu.VMEM((batch, tb, G_WIDTH), BF16),
            pltpu.VMEM((batch, tb, G_HEADS * G_CHUNK), BF16),
            pltpu.VMEM((batch, nchunk * G_HEADS, G_HEAD_DIM, G_CHUNK), BF16),
            pltpu.VMEM((batch, nchunk * G_HEADS, SUBLANES, LANES), F32),
            pltpu.VMEM((batch, G_HEADS, G_HEAD_DIM, G_HEAD_DIM), F32),
        ],
        compiler_params=pltpu.CompilerParams(
            dimension_semantics=("arbitrary",),
            vmem_limit_bytes=VMEM_LIMIT),
        name="gated_deltanet",
    )(proj3, proj3, proj3, proj3, proj3, cwq, cwk, cwv, alog, dtb, nw)
    return out.reshape(batch * seq, G_WIDTH)


OUT_TM = 512


def _outproj_kernel(ya_ref, yr_ref, yg_ref, x_ref, w_ref, g_ref, b_ref, o_ref, ob_ref):
    half = OUT_TM // 2
    for r in range(2):
        rows = slice(r * half, (r + 1) * half)
        y = jnp.dot(ya_ref[rows, :], w_ref[0:A_WIDTH, :], preferred_element_type=F32)
        y += jnp.dot(yr_ref[rows, :], w_ref[A_WIDTH:A_WIDTH + R_WIDTH, :], preferred_element_type=F32)
        y += jnp.dot(yg_ref[rows, :], w_ref[A_WIDTH + R_WIDTH:MIX_WIDTH, :], preferred_element_type=F32)
        z = DEEPNORM_ALPHA * x_ref[rows, :] + y
        mu = jnp.mean(z, axis=-1, keepdims=True)
        zc = z - mu
        var = jnp.mean(zc * zc, axis=-1, keepdims=True)
        out = zc * lax.rsqrt(var + LN_EPS) * g_ref[...] + b_ref[...]
        o_ref[rows, :] = out
        ob_ref[rows, :] = out.astype(BF16)


def _outproj(ya, yr, yg, x, w, layer, g, b):
    T = x.shape[0]
    tm = OUT_TM
    row = lambda width: pl.BlockSpec((tm, width), lambda i: (i, 0))
    vec = lambda: pl.BlockSpec((1, D_MODEL), lambda i: (0, 0))
    return pl.pallas_call(
        _outproj_kernel,
        grid=(T // tm,),
        in_specs=[row(A_WIDTH), row(R_WIDTH), row(G_WIDTH), row(D_MODEL),
                  pl.BlockSpec((None, MIX_WIDTH, D_MODEL), lambda i: (layer, 0, 0),
                               pipeline_mode=pl.Buffered(1)),
                  vec(), vec()],
        out_specs=[row(D_MODEL), row(D_MODEL)],
        out_shape=[jax.ShapeDtypeStruct((T, D_MODEL), F32),
                   jax.ShapeDtypeStruct((T, D_MODEL), BF16)],
        compiler_params=pltpu.CompilerParams(
            dimension_semantics=("parallel",),
            vmem_limit_bytes=VMEM_LIMIT),
        name="outproj_deepnorm",
    )(ya, yr, yg, x, w, g, b)


def _rope_tables(seq):
    half = A_HEAD_DIM // 2
    inv = 1.0 / (ROPE_THETA ** (jnp.arange(0, A_HEAD_DIM, 2, dtype=F32) / A_HEAD_DIM))
    ang = jnp.arange(seq, dtype=F32)[:, None] * inv[None, :]
    cos, sin = jnp.cos(ang), jnp.sin(ang)
    reps = LANES // A_HEAD_DIM
    cos_t = jnp.tile(jnp.concatenate([cos, cos], axis=1), (1, reps))
    sin_t = jnp.tile(jnp.concatenate([-sin, sin], axis=1), (1, reps))
    assert cos_t.shape == (seq, LANES) and half * 2 == A_HEAD_DIM
    return cos_t, sin_t


def _lane_vec(v, offset):
    return jnp.zeros((1, LANES), F32).at[0, offset:offset + v.shape[0]].set(v.astype(F32))


@jax.jit
def _forward(x, w_in, sinks, r_conv_w, r_conv_b, r_wa, r_ba, r_wx, r_bx, r_lam,
             g_conv_w, g_a_log, g_dt_bias, g_norm_w, w_out, ln_g, ln_b):
    batch, seq, _ = x.shape
    T = batch * seq
    cos_t, sin_t = _rope_tables(seq)
    w_out_b = w_out.astype(BF16)
    w_in_t = jnp.swapaxes(w_in, 1, 2)
    xf = x.reshape(T, D_MODEL)
    xb = xf.astype(BF16)
    for l in range(DEPTH):
        proj = _inproj(xb, w_in_t, l)
        ya = _attention(proj, sinks[l], cos_t, sin_t, batch, seq)
        yr = _rglru(proj, r_conv_w[l], r_conv_b[l][None, :], r_wa[l].astype(BF16), r_ba[l][None, :],
                    r_wx[l].astype(BF16), r_bx[l][None, :], r_lam[l][None, :], batch, seq)
        gcw = g_conv_w[l]
        yg = _gdn(proj, gcw[:, :G_WIDTH], gcw[:, G_WIDTH:2 * G_WIDTH], gcw[:, 2 * G_WIDTH:],
                  _lane_vec(g_a_log[l], G_HEADS), _lane_vec(g_dt_bias[l], G_HEADS),
                  g_norm_w[l][None, :], batch, seq)
        xf, xb = _outproj(ya, yr, yg, xf, w_out_b, l, ln_g[l][None, :], ln_b[l][None, :])
    return xf.reshape(batch, seq, D_MODEL)


def kernel(x, w_in, sinks, r_conv_w, r_conv_b, r_wa, r_ba, r_wx, r_bx, r_lam, g_conv_w, g_a_log,
           g_dt_bias, g_norm_w, w_out, ln_g, ln_b):
    return _forward(x, w_in, sinks, r_conv_w, r_conv_b, r_wa, r_ba, r_wx, r_bx, r_lam,
                    g_conv_w, g_a_log, g_dt_bias, g_norm_w, w_out, ln_g, ln_b)
```

```python
import functools
import math

import numpy as np
import jax
import jax.numpy as jnp
from jax import lax
from jax.experimental import pallas as pl
from jax.experimental.pallas import tpu as pltpu

D_MODEL = 2048
DEPTH = 2
A_HEADS = 8
A_KV_HEADS = 2
A_HEAD_DIM = 64
A_WIDTH = A_HEADS * A_HEAD_DIM
A_KV_WIDTH = A_KV_HEADS * A_HEAD_DIM
WINDOW = 128
A_BLOCK = 128
ROPE_THETA = 10000.0
R_WIDTH = 1024
R_BLOCKS = 8
R_BLOCK_DIM = R_WIDTH // R_BLOCKS
R_C = 8.0
CONV_WIDTH = 4
G_HEADS = 4
G_HEAD_DIM = 128
G_WIDTH = G_HEADS * G_HEAD_DIM
G_CHUNK = 64
MIX_WIDTH = A_WIDTH + R_WIDTH + G_WIDTH
DEEPNORM_ALPHA = (2 * DEPTH) ** 0.25
LN_EPS = 1e-5
RMS_EPS = 1e-6

LANES = 128
SUBLANES = 8
VMEM_LIMIT = 56 * 1024 * 1024

COL_RX = 0
COL_RZ = 1024
COL_AQ = 2048
COL_AZ = 2560
COL_GQ = 3072
COL_GK = 3584
COL_GV = 4096
COL_GZ = 4608
COL_KV = 5120
COL_SMALL = 5376
N_PROJ = 5632

F32 = jnp.float32
BF16 = jnp.bfloat16
HIGHEST = lax.Precision.HIGHEST


def _sigmoid(x):
    return 0.5 * jnp.tanh(0.5 * x) + 0.5


def _silu(x):
    return x * _sigmoid(x)


def _softplus(x):
    return jnp.maximum(x, 0.0) + jnp.log1p(jnp.exp(-jnp.abs(x)))


N_IN = 5384
IN_TM = 2048
IN_TN = 512
IN_SUB = 256
IN_TILE_ORDER = (5, 6, 7, 8, 9, 10, 11, 12, 0, 1, 3, 4, 13, 14, 15, 16, 17, 18, 19, 20, 2, 21)


def _inproj_kernel(order_ref, x_ref, w0_ref, w1_ref, o_ref, wb):
    j = pl.program_id(0)

    @pl.when(pl.program_id(1) == 0)
    def _():
        row = lax.broadcasted_iota(jnp.int32, (IN_SUB, D_MODEL), 0)
        for t, w_ref in enumerate((w0_ref, w1_ref)):
            valid = N_IN - order_ref[2 * j + t] * IN_SUB
            wb[t * IN_SUB:(t + 1) * IN_SUB, :] = jnp.where(row < valid, w_ref[...], 0.0).astype(BF16)

    o_ref[...] = lax.dot_general(x_ref[...], wb[...], (((1,), (1,)), ((), ())),
                                 preferred_element_type=F32)


def _inproj(xb, w_in_t, layer):
    T = xb.shape[0]
    per_step = IN_TN // IN_SUB
    assert per_step == 2 and len(IN_TILE_ORDER) * IN_SUB == N_PROJ
    order = jnp.asarray(IN_TILE_ORDER, jnp.int32)
    w_spec = lambda t: pl.BlockSpec((None, IN_SUB, D_MODEL),
                                    lambda j, i, order_ref: (layer, order_ref[per_step * j + t], 0))
    return pl.pallas_call(
        _inproj_kernel,
        grid_spec=pltpu.PrefetchScalarGridSpec(
            num_scalar_prefetch=1,
            grid=(N_PROJ // IN_TN, T // IN_TM),
            in_specs=[pl.BlockSpec((IN_TM, D_MODEL), lambda j, i, order_ref: (i, 0)),
                      w_spec(0), w_spec(1)],
            out_specs=pl.BlockSpec((IN_TM, IN_TN), lambda j, i, order_ref: (i, j)),
            scratch_shapes=[pltpu.VMEM((IN_TN, D_MODEL), BF16)]),
        out_shape=jax.ShapeDtypeStruct((T, N_PROJ), F32),
        compiler_params=pltpu.CompilerParams(
            dimension_semantics=("arbitrary", "arbitrary"),
            vmem_limit_bytes=VMEM_LIMIT),
        name="inproj",
    )(order, xb, w_in_t, w_in_t)


def _rope(x, cos, sin_signed):
    w = x.shape[1]
    reps = w // LANES
    if reps > 1:
        cos = jnp.concatenate([cos] * reps, axis=1)
        sin_signed = jnp.concatenate([sin_signed] * reps, axis=1)
    lane = lax.broadcasted_iota(jnp.int32, x.shape, 1)
    first_half = (lane % A_HEAD_DIM) < (A_HEAD_DIM // 2)
    swapped = jnp.where(first_half,
                        pltpu.roll(x, w - A_HEAD_DIM // 2, 1),
                        pltpu.roll(x, A_HEAD_DIM // 2, 1))
    return x * cos + swapped * sin_signed


A_TQ = 128
A_HALF = LANES // 2


def _attn_kernel(sinks_ref, q_ref, kv_ref, az_ref, cos_ref, sin_ref, o_ref, prev):
    step = pl.program_id(1)
    nblk = A_WIDTH // LANES

    @pl.when(step == 0)
    def _():
        prev[...] = jnp.zeros_like(prev)

    cos = cos_ref[...]
    sin = sin_ref[...]
    q = (_rope(q_ref[...], cos, sin) * (A_HEAD_DIM ** -0.5)).astype(BF16)
    kv = kv_ref[...]
    k = _rope(kv[:, :A_KV_WIDTH], cos, sin)
    v = kv[:, A_KV_WIDTH:]
    kb = k.astype(BF16)
    vb = v.astype(BF16)
    kswb = pltpu.roll(k, A_HALF, 1).astype(BF16)
    vswb = pltpu.roll(v, A_HALF, 1).astype(BF16)

    rows4 = nblk * A_BLOCK
    qi = lax.broadcasted_iota(jnp.int32, (rows4, 2 * A_BLOCK), 0) % A_BLOCK
    kj = lax.broadcasted_iota(jnp.int32, (rows4, 2 * A_BLOCK), 1)
    diff = qi - kj + A_BLOCK
    band = (diff >= 0) & (diff < WINDOW)
    slab = lax.broadcasted_iota(jnp.int32, (rows4, 1), 0) // A_BLOCK
    lo = lax.broadcasted_iota(jnp.int32, (A_BLOCK, LANES), 1) < A_HALF
    ones = jnp.ones((2 * A_BLOCK, LANES), BF16)

    def softmax_pv(lhs, k_all, v_aug, heads, mask):
        s = lax.dot_general(lhs, k_all, (((1,), (1,)), ((), ())), preferred_element_type=F32)
        s = jnp.where(mask, s, -jnp.inf)
        sink = jnp.zeros((rows4, 1), F32)
        for n, h in enumerate(heads):
            sink = jnp.where(slab == n, sinks_ref[h], sink)
        m = jnp.maximum(jnp.max(s, axis=-1, keepdims=True), sink)
        p = jnp.exp(s - m).astype(BF16)
        o = jnp.dot(p, v_aug, preferred_element_type=F32)
        den = o[:, LANES:] + jnp.exp(sink - m)
        return o[:, :LANES] / den

    kp, kpsw, vp, vpsw = prev[0], prev[1], prev[2], prev[3]
    for j in range(A_TQ // A_BLOCK):
        rows = slice(j * A_BLOCK, (j + 1) * A_BLOCK)
        kc, kcsw, vc, vcsw = kb[rows], kswb[rows], vb[rows], vswb[rows]
        k_all = jnp.concatenate([kp, kc], axis=0)
        k_all_sw = jnp.concatenate([kpsw, kcsw], axis=0)
        v_aug = jnp.concatenate([jnp.concatenate([vp, vc], axis=0), ones], axis=1)
        v_aug_sw = jnp.concatenate([jnp.concatenate([vpsw, vcsw], axis=0), ones], axis=1)
        mask = band if j > 0 else band & ((kj >= A_BLOCK) | (step > 0))

        qj = q[rows]
        blocks = [qj[:, n * LANES:(n + 1) * LANES] for n in range(nblk)]
        zero = jnp.zeros_like(blocks[0])
        q_lo = [jnp.where(lo, x, zero) for x in blocks]
        q_hi = [jnp.where(lo, zero, x) for x in blocks]
        half = nblk // A_KV_HEADS
        lhs_a = jnp.concatenate(q_lo[:half] + q_hi[half:], axis=0)
        lhs_b = jnp.concatenate(q_hi[:half] + q_lo[half:], axis=0)
        heads_a = [2 * n for n in range(half)] + [2 * n + 1 for n in range(half, nblk)]
        heads_b = [2 * n + 1 for n in range(half)] + [2 * n for n in range(half, nblk)]
        oa = softmax_pv(lhs_a, k_all, v_aug, heads_a, mask)
        ob = softmax_pv(lhs_b, k_all_sw, v_aug_sw, heads_b, mask)
        outs = []
        for n in range(nblk):
            a_n = oa[n * A_BLOCK:(n + 1) * A_BLOCK]
            b_n = ob[n * A_BLOCK:(n + 1) * A_BLOCK]
            outs.append(jnp.where(lo, a_n, b_n) if n < half else jnp.where(lo, b_n, a_n))
        out = jnp.concatenate(outs, axis=1)
        o_ref[rows, :] = (out * _silu(az_ref[rows, :])).astype(BF16)
        kp, kpsw, vp, vpsw = kc, kcsw, vc, vcsw

    prev[0] = kp
    prev[1] = kpsw
    prev[2] = vp
    prev[3] = vpsw


def _attention(proj, sinks, cos_t, sin_t, batch, seq):
    nq = seq // A_TQ
    T = batch * seq
    assert (A_HEAD_DIM * 2 == LANES and A_KV_WIDTH == LANES
            and (A_HEADS // A_KV_HEADS) % 2 == 0 and WINDOW == A_BLOCK)
    return pl.pallas_call(
        _attn_kernel,
        grid=(batch, nq),
        in_specs=[
            pl.BlockSpec(memory_space=pltpu.SMEM),
            pl.BlockSpec((A_TQ, A_WIDTH), lambda b, i: (b * nq + i, COL_AQ // A_WIDTH)),
            pl.BlockSpec((A_TQ, 2 * A_KV_WIDTH), lambda b, i: (b * nq + i, COL_KV // (2 * A_KV_WIDTH))),
            pl.BlockSpec((A_TQ, A_WIDTH), lambda b, i: (b * nq + i, COL_AZ // A_WIDTH)),
            pl.BlockSpec((A_TQ, LANES), lambda b, i: (i, 0)),
            pl.BlockSpec((A_TQ, LANES), lambda b, i: (i, 0)),
        ],
        out_specs=pl.BlockSpec((A_TQ, A_WIDTH), lambda b, i: (b * nq + i, 0)),
        out_shape=jax.ShapeDtypeStruct((T, A_WIDTH), BF16),
        scratch_shapes=[pltpu.VMEM((4, A_BLOCK, LANES), BF16)],
        compiler_params=pltpu.CompilerParams(
            dimension_semantics=("parallel", "arbitrary"),
            vmem_limit_bytes=VMEM_LIMIT),
        name="swa_attention",
    )(sinks, proj, proj, proj, cos_t, sin_t)


R_TB = 256
SQRT_FLOOR = 1e-30


def _causal_conv(buf_ref, w, tb):
    xext = buf_ref[0:SUBLANES + tb, :]
    acc = xext[SUBLANES:, :] * w[CONV_WIDTH - 1:CONV_WIDTH, :]
    for k in range(CONV_WIDTH - 1):
        shifted = pltpu.roll(xext, CONV_WIDTH - 1 - k, 0)[SUBLANES:, :]
        acc = acc + shifted * w[k:k + 1, :]
    return acc


def _rglru_kernel(rx_ref, rz_ref, cw_ref, cb_ref, wa_ref, ba_ref, wx_ref, bx_ref, lam_ref,
                  o_ref, xbuf, acum, ucum, hcarry):
    tb = R_TB

    @pl.when(pl.program_id(1) == 0)
    def _():
        xbuf[0:SUBLANES, :] = jnp.zeros((SUBLANES, R_WIDTH), F32)
        hcarry[...] = jnp.zeros_like(hcarry)

    xbuf[SUBLANES:SUBLANES + tb, :] = rx_ref[...]
    xr = _causal_conv(xbuf, cw_ref[...], tb) + cb_ref[...]
    xbuf[0:SUBLANES, :] = xbuf[tb:tb + SUBLANES, :]

    xrb = xr.astype(BF16)
    ra, ia = [], []
    for n in range(R_BLOCKS):
        xb = xrb[:, n * R_BLOCK_DIM:(n + 1) * R_BLOCK_DIM]
        ra.append(jnp.dot(xb, wa_ref[n], preferred_element_type=F32))
        ia.append(jnp.dot(xb, wx_ref[n], preferred_element_type=F32))
    r = _sigmoid(jnp.concatenate(ra, axis=1) + ba_ref[...])
    ig = _sigmoid(jnp.concatenate(ia, axis=1) + bx_ref[...])
    log_a = (-R_C) * r * _softplus(-lam_ref[...])
    a = jnp.exp(log_a)
    s = jnp.maximum(1.0 - a * a, 0.0)
    u = (s * lax.rsqrt(jnp.maximum(s, SQRT_FLOOR))) * (ig * xr)

    row8 = lax.broadcasted_iota(jnp.int32, (tb, R_WIDTH), 0) % SUBLANES
    for k in (1, 2, 4):
        keep = row8 >= k
        a_sh = jnp.where(keep, pltpu.roll(a, k, 0), 1.0)
        u_sh = jnp.where(keep, pltpu.roll(u, k, 0), 0.0)
        u = a * u_sh + u
        a = a * a_sh
    acum[...] = a
    ucum[...] = u

    def body(g, hprev):
        r0 = pl.multiple_of(g * SUBLANES, SUBLANES)
        h = acum[pl.ds(r0, SUBLANES), :] * hprev + ucum[pl.ds(r0, SUBLANES), :]
        ucum[pl.ds(r0, SUBLANES), :] = h
        return jnp.broadcast_to(h[SUBLANES - 1:SUBLANES, :], (SUBLANES, R_WIDTH))

    hlast = lax.fori_loop(0, tb // SUBLANES, body, hcarry[...])
    hcarry[...] = hlast
    o_ref[...] = (ucum[...] * _silu(rz_ref[...])).astype(BF16)


def _rglru(proj, cw, cb, wa, ba, wx, bx, lam, batch, seq):
    tb = R_TB
    nblk = seq // tb
    T = batch * seq
    vec = lambda: pl.BlockSpec((1, R_WIDTH), lambda b, i: (0, 0))
    return pl.pallas_call(
        _rglru_kernel,
        grid=(batch, nblk),
        in_specs=[
            pl.BlockSpec((tb, R_WIDTH), lambda b, i: (b * nblk + i, COL_RX // R_WIDTH)),
            pl.BlockSpec((tb, R_WIDTH), lambda b, i: (b * nblk + i, COL_RZ // R_WIDTH)),
            pl.BlockSpec((CONV_WIDTH, R_WIDTH), lambda b, i: (0, 0)),
            vec(),
            pl.BlockSpec((R_BLOCKS, R_BLOCK_DIM, R_BLOCK_DIM), lambda b, i: (0, 0, 0)),
            vec(),
            pl.BlockSpec((R_BLOCKS, R_BLOCK_DIM, R_BLOCK_DIM), lambda b, i: (0, 0, 0)),
            vec(),
            vec(),
        ],
        out_specs=pl.BlockSpec((tb, R_WIDTH), lambda b, i: (b * nblk + i, 0)),
        out_shape=jax.ShapeDtypeStruct((T, R_WIDTH), BF16),
        scratch_shapes=[
            pltpu.VMEM((SUBLANES + tb, R_WIDTH), F32),
            pltpu.VMEM((tb, R_WIDTH), F32),
            pltpu.VMEM((tb, R_WIDTH), F32),
            pltpu.VMEM((SUBLANES, R_WIDTH), F32),
        ],
        compiler_params=pltpu.CompilerParams(
            dimension_semantics=("parallel", "arbitrary"),
            vmem_limit_bytes=VMEM_LIMIT),
        name="rglru",
    )(proj, proj, cw, cb, wa, ba, wx, bx, lam)


G_TB = 512


def _l2norm(x):
    return x * lax.rsqrt(jnp.sum(x * x, axis=-1, keepdims=True) + RMS_EPS)


def _bdot(a, b):
    return jnp.dot(a.astype(BF16), b.astype(BF16), preferred_element_type=F32)


def _unit_lower_solves(ms, rhss):
    c = ms[0].shape[0]
    eye = (lax.broadcasted_iota(jnp.int32, (c, c), 0) ==
           lax.broadcasted_iota(jnp.int32, (c, c), 1)).astype(F32)
    p_s = [eye - m for m in ms]
    a_s = [_bdot(m, m) for m in ms]
    k = 2
    while k < c // 2:
        st = [_bdot(jnp.concatenate([a, p], axis=0), a) for a, p in zip(a_s, p_s)]
        p_s = [p + s[c:] for p, s in zip(p_s, st)]
        a_s = [s[:c] for s in st]
        k *= 2
    ys = [r + _bdot(a, r) for a, r in zip(a_s, rhss)]
    return [_bdot(p, y) for p, y in zip(p_s, ys)]


G_GROUP = 2


def _gdn_kernel(gq_ref, gk_ref, gv_ref, gz_ref, sm_ref, cwq_ref, cwk_ref, cwv_ref,
                alog_ref, dtb_ref, nw_ref, o_ref,
                qbuf, kbuf, vbuf, qs, ks, vs, bg, us, ws, qds, qks, kdts, gls, state):
    tb = G_TB
    C = G_CHUNK
    nb = gq_ref.shape[0]

    @pl.when(pl.program_id(0) == 0)
    def _():
        zeros = jnp.zeros((nb, SUBLANES, G_WIDTH), F32)
        qbuf[:, 0:SUBLANES, :] = zeros
        kbuf[:, 0:SUBLANES, :] = zeros
        vbuf[:, 0:SUBLANES, :] = zeros
        state[...] = jnp.zeros_like(state)

    lane = lax.broadcasted_iota(jnp.int32, (tb, LANES), 1)
    for b in range(nb):
        for src, buf, cw, dst in ((gq_ref, qbuf, cwq_ref, qs), (gk_ref, kbuf, cwk_ref, ks),
                                  (gv_ref, vbuf, cwv_ref, vs)):
            buf[b, SUBLANES:SUBLANES + tb, :] = src[b]
            dst[b] = _silu(_causal_conv(buf.at[b], cw[...], tb))
            buf[b, 0:SUBLANES, :] = buf[b, tb:tb + SUBLANES, :]
        small = sm_ref[b]
        g_all = -jnp.exp(alog_ref[...]) * _softplus(small + dtb_ref[...])
        bg[b] = jnp.where(lane < G_HEADS, _sigmoid(small), g_all)

    ri = lax.broadcasted_iota(jnp.int32, (C, C), 0)
    ci = lax.broadcasted_iota(jnp.int32, (C, C), 1)
    tril = ri >= ci
    strict = ri > ci
    tril_f = tril.astype(F32)

    def intra(grp, carry):
        dests, ms, rhss = [], [], []
        for b in range(nb):
            for cc in range(G_GROUP):
                c = grp * G_GROUP + cc
                r0 = pl.multiple_of(c * C, C)
                bgc = bg[b, pl.ds(r0, C), :]
                gcum_col = jnp.dot(tril_f, bgc, precision=HIGHEST, preferred_element_type=F32)
                gcum_row = gcum_col.T[:SUBLANES, :]
                for h in range(G_HEADS):
                    cols = slice(h * G_HEAD_DIM, (h + 1) * G_HEAD_DIM)
                    qn = _l2norm(qs[b, pl.ds(r0, C), cols]) * (G_HEAD_DIM ** -0.5)
                    kn = _l2norm(ks[b, pl.ds(r0, C), cols])
                    v = vs[b, pl.ds(r0, C), cols]
                    beta = jnp.broadcast_to(bgc[:, h:h + 1], (C, G_HEAD_DIM))
                    gc = jnp.broadcast_to(gcum_col[:, G_HEADS + h:G_HEADS + h + 1], (C, G_HEAD_DIM))
                    gr = gcum_row[G_HEADS + h:G_HEADS + h + 1, :]
                    g_last = gc[C - 1:C, :]
                    decay = jnp.where(tril, jnp.exp(gc[:, :C] - gr), 0.0)
                    eg = jnp.exp(gc)
                    knb = kn.astype(BF16)
                    qk_kk = lax.dot_general(jnp.concatenate([qn.astype(BF16), knb], axis=0), knb,
                                            (((1,), (1,)), ((), ())), preferred_element_type=F32)
                    qks[b, pl.ds(r0, C), h * C:(h + 1) * C] = (qk_kk[:C] * decay).astype(BF16)
                    ms.append(jnp.where(strict, qk_kk[C:] * beta[:, :C] * decay, 0.0))
                    qds[b, pl.ds(r0, C), cols] = (qn * eg).astype(BF16)
                    kdts[b, c * G_HEADS + h] = (kn * jnp.exp(g_last - gc)).T.astype(BF16)
                    gls[b, c * G_HEADS + h] = jnp.broadcast_to(jnp.exp(g_last), (SUBLANES, LANES))
                    rhss.append(jnp.concatenate([v * beta, kn * (beta * eg)], axis=1))
                    dests.append((b, r0, cols))
        for (b, r0, cols), uw in zip(dests, _unit_lower_solves(ms, rhss)):
            us[b, pl.ds(r0, C), cols] = uw[:, :G_HEAD_DIM]
            ws[b, pl.ds(r0, C), cols] = uw[:, G_HEAD_DIM:].astype(BF16)
        return carry

    for grp in range(tb // (C * G_GROUP)):
        intra(grp, 0)

    chains = [(b, h) for b in range(nb) for h in range(G_HEADS)]

    def inter(c, carry):
        r0 = pl.multiple_of(c * C, C)
        rows = pl.ds(r0, C)
        hcols = lambda h: slice(h * G_HEAD_DIM, (h + 1) * G_HEAD_DIM)
        s_old = [state[b, h] for b, h in chains]
        sb = [s.astype(BF16) for s in s_old]
        w_s = [jnp.dot(ws[b, rows, hcols(h)], s, preferred_element_type=F32)
               for (b, h), s in zip(chains, sb)]
        vnb = [(us[b, rows, hcols(h)] - x).astype(BF16) for (b, h), x in zip(chains, w_s)]
        upd = [jnp.dot(kdts[b, c * G_HEADS + h], v, preferred_element_type=F32)
               for (b, h), v in zip(chains, vnb)]
        for (b, h), s, x in zip(chains, s_old, upd):
            state[b, h] = s * gls[b, c * G_HEADS + h][0:1, :] + x
        for (b, h), s, v in zip(chains, sb, vnb):
            o = (jnp.dot(qds[b, rows, hcols(h)], s, preferred_element_type=F32)
                 + jnp.dot(qks[b, rows, h * C:(h + 1) * C], v, preferred_element_type=F32))
            on = o * lax.rsqrt(jnp.mean(o * o, axis=-1, keepdims=True) + RMS_EPS) * nw_ref[...]
            o_ref[b, rows, hcols(h)] = (on * _silu(gz_ref[b, rows, hcols(h)])).astype(BF16)
        return carry

    lax.fori_loop(0, tb // C, inter, 0, unroll=True)


def _gdn(proj, cwq, cwk, cwv, alog, dtb, nw, batch, seq):
    tb = G_TB
    nchunk = tb // G_CHUNK
    proj3 = proj.reshape(batch, seq, N_PROJ)
    seg = lambda col: pl.BlockSpec((batch, tb, G_WIDTH), lambda i: (0, i, col // G_WIDTH))
    cw = lambda: pl.BlockSpec((CONV_WIDTH, G_WIDTH), lambda i: (0, 0))
    vec = lambda: pl.BlockSpec((1, LANES), lambda i: (0, 0))
    out = pl.pallas_call(
        _gdn_kernel,
        grid=(seq // tb,),
        in_specs=[seg(COL_GQ), seg(COL_GK), seg(COL_GV), seg(COL_GZ),
                  pl.BlockSpec((batch, tb, LANES), lambda i: (0, i, COL_SMALL // LANES)),
                  cw(), cw(), cw(), vec(), vec(), vec()],
        out_specs=pl.BlockSpec((batch, tb, G_WIDTH), lambda i: (0, i, 0)),
        out_shape=jax.ShapeDtypeStruct((batch, seq, G_WIDTH), BF16),
        scratch_shapes=[
            pltpu.VMEM((batch, SUBLANES + tb, G_WIDTH), F32),
            pltpu.VMEM((batch, SUBLANES + tb, G_WIDTH), F32),
            pltpu.VMEM((batch, SUBLANES + tb, G_WIDTH), F32),
            pltpu.VMEM((batch, tb, G_WIDTH), F32),
            pltpu.VMEM((batch, tb, G_WIDTH), F32),
            pltpu.VMEM((batch, tb, G_WIDTH), F32),
            pltpu.VMEM((batch, tb, LANES), F32),
            pltpu.VMEM((batch, tb, G_WIDTH), F32),
            pltpu.VMEM((batch, tb, G_WIDTH), BF16),
            pltpu.VMEM((batch, tb, G_WIDTH), BF16),
            pltpu.VMEM((batch, tb, G_HEADS * G_CHUNK), BF16),
            pltpu.VMEM((batch, nchunk * G_HEADS, G_HEAD_DIM, G_CHUNK), BF16),
            pltpu.VMEM((batch, nchunk * G_HEADS, SUBLANES, LANES), F32),
            pltpu.VMEM((batch, G_HEADS, G_HEAD_DIM, G_HEAD_DIM), F32),
        ],
        compiler_params=pltpu.CompilerParams(
            dimension_semantics=("arbitrary",),
            vmem_limit_bytes=VMEM_LIMIT),
        name="gated_deltanet",
    )(proj3, proj3, proj3, proj3, proj3, cwq, cwk, cwv, alog, dtb, nw)
    return out.reshape(batch * seq, G_WIDTH)


OUT_TM = 512


def _outproj_kernel(ya_ref, yr_ref, yg_ref, x_ref, w_ref, g_ref, b_ref, o_ref, ob_ref, wb):
    @pl.when(pl.program_id(0) == 0)
    def _():
        wb[...] = w_ref[...].astype(BF16)

    half = OUT_TM // 2
    for r in range(2):
        rows = slice(r * half, (r + 1) * half)
        y = jnp.dot(ya_ref[rows, :], wb[0:A_WIDTH, :], preferred_element_type=F32)
        y += jnp.dot(yr_ref[rows, :], wb[A_WIDTH:A_WIDTH + R_WIDTH, :], preferred_element_type=F32)
        y += jnp.dot(yg_ref[rows, :], wb[A_WIDTH + R_WIDTH:MIX_WIDTH, :], preferred_element_type=F32)
        z = DEEPNORM_ALPHA * x_ref[rows, :] + y
        mu = jnp.mean(z, axis=-1, keepdims=True)
        zc = z - mu
        var = jnp.mean(zc * zc, axis=-1, keepdims=True)
        out = zc * lax.rsqrt(var + LN_EPS) * g_ref[...] + b_ref[...]
        o_ref[rows, :] = out
        ob_ref[rows, :] = out.astype(BF16)


def _outproj(ya, yr, yg, x, w, layer, g, b):
    T = x.shape[0]
    tm = OUT_TM
    row = lambda width: pl.BlockSpec((tm, width), lambda i: (i, 0))
    vec = lambda: pl.BlockSpec((1, D_MODEL), lambda i: (0, 0))
    return pl.pallas_call(
        _outproj_kernel,
        grid=(T // tm,),
        in_specs=[row(A_WIDTH), row(R_WIDTH), row(G_WIDTH), row(D_MODEL),
                  pl.BlockSpec((None, MIX_WIDTH, D_MODEL), lambda i: (layer, 0, 0),
                               pipeline_mode=pl.Buffered(1)),
                  vec(), vec()],
        out_specs=[row(D_MODEL), row(D_MODEL)],
        out_shape=[jax.ShapeDtypeStruct((T, D_MODEL), F32),
                   jax.ShapeDtypeStruct((T, D_MODEL), BF16)],
        scratch_shapes=[pltpu.VMEM((MIX_WIDTH, D_MODEL), BF16)],
        compiler_params=pltpu.CompilerParams(
            dimension_semantics=("arbitrary",),
            vmem_limit_bytes=VMEM_LIMIT),
        name="outproj_deepnorm",
    )(ya, yr, yg, x, w, g, b)


def _rope_tables(seq):
    half = A_HEAD_DIM // 2
    inv = 1.0 / (ROPE_THETA ** (jnp.arange(0, A_HEAD_DIM, 2, dtype=F32) / A_HEAD_DIM))
    ang = jnp.arange(seq, dtype=F32)[:, None] * inv[None, :]
    cos, sin = jnp.cos(ang), jnp.sin(ang)
    reps = LANES // A_HEAD_DIM
    cos_t = jnp.tile(jnp.concatenate([cos, cos], axis=1), (1, reps))
    sin_t = jnp.tile(jnp.concatenate([-sin, sin], axis=1), (1, reps))
    assert cos_t.shape == (seq, LANES) and half * 2 == A_HEAD_DIM
    return cos_t, sin_t


def _lane_vec(v, offset):
    return jnp.zeros((1, LANES), F32).at[0, offset:offset + v.shape[0]].set(v.astype(F32))


@jax.jit
def _forward(x, w_in, sinks, r_conv_w, r_conv_b, r_wa, r_ba, r_wx, r_bx, r_lam,
             g_conv_w, g_a_log, g_dt_bias, g_norm_w, w_out, ln_g, ln_b):
    batch, seq, _ = x.shape
    T = batch * seq
    cos_t, sin_t = _rope_tables(seq)
    w_in_t = jnp.swapaxes(w_in, 1, 2)
    xf = x.reshape(T, D_MODEL)
    xb = xf.astype(BF16)
    for l in range(DEPTH):
        proj = _inproj(xb, w_in_t, l)
        ya = _attention(proj, sinks[l], cos_t, sin_t, batch, seq)
        yr = _rglru(proj, r_conv_w[l], r_conv_b[l][None, :], r_wa[l].astype(BF16), r_ba[l][None, :],
                    r_wx[l].astype(BF16), r_bx[l][None, :], r_lam[l][None, :], batch, seq)
        gcw = g_conv_w[l]
        yg = _gdn(proj, gcw[:, :G_WIDTH], gcw[:, G_WIDTH:2 * G_WIDTH], gcw[:, 2 * G_WIDTH:],
                  _lane_vec(g_a_log[l], G_HEADS), _lane_vec(g_dt_bias[l], G_HEADS),
                  g_norm_w[l][None, :], batch, seq)
        xf, xb = _outproj(ya, yr, yg, xf, w_out, l, ln_g[l][None, :], ln_b[l][None, :])
    return xf.reshape(batch, seq, D_MODEL)


def kernel(x, w_in, sinks, r_conv_w, r_conv_b, r_wa, r_ba, r_wx, r_bx, r_lam, g_conv_w, g_a_log,
           g_dt_bias, g_norm_w, w_out, ln_g, ln_b):
    return _forward(x, w_in, sinks, r_conv_w, r_conv_b, r_wa, r_ba, r_wx, r_bx, r_lam,
                    g_conv_w, g_a_log, g_dt_bias, g_norm_w, w_out, ln_g, ln_b)
```

```python
import functools
import math

import numpy as np
import jax
import jax.numpy as jnp
from jax import lax
from jax.experimental import pallas as pl
from jax.experimental.pallas import tpu as pltpu

D_MODEL = 2048
DEPTH = 2
A_HEADS = 8
A_KV_HEADS = 2
A_HEAD_DIM = 64
A_WIDTH = A_HEADS * A_HEAD_DIM
A_KV_WIDTH = A_KV_HEADS * A_HEAD_DIM
WINDOW = 128
A_BLOCK = 128
ROPE_THETA = 10000.0
R_WIDTH = 1024
R_BLOCKS = 8
R_BLOCK_DIM = R_WIDTH // R_BLOCKS
R_C = 8.0
CONV_WIDTH = 4
G_HEADS = 4
G_HEAD_DIM = 128
G_WIDTH = G_HEADS * G_HEAD_DIM
G_CHUNK = 64
MIX_WIDTH = A_WIDTH + R_WIDTH + G_WIDTH
DEEPNORM_ALPHA = (2 * DEPTH) ** 0.25
LN_EPS = 1e-5
RMS_EPS = 1e-6

LANES = 128
SUBLANES = 8
VMEM_LIMIT = 56 * 1024 * 1024

COL_RX = 0
COL_RZ = 1024
COL_AQ = 2048
COL_AZ = 2560
COL_GQ = 3072
COL_GK = 3584
COL_GV = 4096
COL_GZ = 4608
COL_KV = 5120
COL_SMALL = 5376
N_PROJ = 5632

F32 = jnp.float32
BF16 = jnp.bfloat16
HIGHEST = lax.Precision.HIGHEST


def _sigmoid(x):
    return 0.5 * jnp.tanh(0.5 * x) + 0.5


def _silu(x):
    return x * _sigmoid(x)


def _softplus(x):
    return jnp.maximum(x, 0.0) + jnp.log1p(jnp.exp(-jnp.abs(x)))


N_IN = 5384
IN_TM = 2048
IN_TN = 512
IN_SUB = 256
IN_TILE_ORDER = (5, 6, 7, 8, 9, 10, 11, 12, 0, 1, 3, 4, 13, 14, 15, 16, 17, 18, 19, 20, 2, 21)


def _inproj_kernel(order_ref, x_ref, w0_ref, w1_ref, o_ref, wb):
    j = pl.program_id(0)

    @pl.when(pl.program_id(1) == 0)
    def _():
        row = lax.broadcasted_iota(jnp.int32, (IN_SUB, D_MODEL), 0)
        for t, w_ref in enumerate((w0_ref, w1_ref)):
            valid = N_IN - order_ref[2 * j + t] * IN_SUB
            wb[t * IN_SUB:(t + 1) * IN_SUB, :] = jnp.where(row < valid, w_ref[...], 0.0).astype(BF16)

    o_ref[...] = lax.dot_general(x_ref[...], wb[...], (((1,), (1,)), ((), ())),
                                 preferred_element_type=F32)


def _inproj(xb, w_in_t, layer):
    T = xb.shape[0]
    per_step = IN_TN // IN_SUB
    assert per_step == 2 and len(IN_TILE_ORDER) * IN_SUB == N_PROJ
    order = jnp.asarray(IN_TILE_ORDER, jnp.int32)
    w_spec = lambda t: pl.BlockSpec((None, IN_SUB, D_MODEL),
                                    lambda j, i, order_ref: (layer, order_ref[per_step * j + t], 0))
    return pl.pallas_call(
        _inproj_kernel,
        grid_spec=pltpu.PrefetchScalarGridSpec(
            num_scalar_prefetch=1,
            grid=(N_PROJ // IN_TN, T // IN_TM),
            in_specs=[pl.BlockSpec((IN_TM, D_MODEL), lambda j, i, order_ref: (i, 0)),
                      w_spec(0), w_spec(1)],
            out_specs=pl.BlockSpec((IN_TM, IN_TN), lambda j, i, order_ref: (i, j)),
            scratch_shapes=[pltpu.VMEM((IN_TN, D_MODEL), BF16)]),
        out_shape=jax.ShapeDtypeStruct((T, N_PROJ), F32),
        compiler_params=pltpu.CompilerParams(
            dimension_semantics=("arbitrary", "arbitrary"),
            vmem_limit_bytes=VMEM_LIMIT),
        name="inproj",
    )(order, xb, w_in_t, w_in_t)


def _rope(x, cos, sin_signed):
    w = x.shape[1]
    reps = w // LANES
    if reps > 1:
        cos = jnp.concatenate([cos] * reps, axis=1)
        sin_signed = jnp.concatenate([sin_signed] * reps, axis=1)
    lane = lax.broadcasted_iota(jnp.int32, x.shape, 1)
    first_half = (lane % A_HEAD_DIM) < (A_HEAD_DIM // 2)
    swapped = jnp.where(first_half,
                        pltpu.roll(x, w - A_HEAD_DIM // 2, 1),
                        pltpu.roll(x, A_HEAD_DIM // 2, 1))
    return x * cos + swapped * sin_signed


A_TQ = 128
A_HALF = LANES // 2


def _attn_kernel(sinks_ref, q_ref, kv_ref, az_ref, cos_ref, sin_ref, bias_ref, o_ref, prev):
    step = pl.program_id(1)
    nblk = A_WIDTH // LANES

    @pl.when(step == 0)
    def _():
        prev[...] = jnp.zeros_like(prev)

    cos = cos_ref[...]
    sin = sin_ref[...]
    q = (_rope(q_ref[...], cos, sin) * (A_HEAD_DIM ** -0.5)).astype(BF16)
    kv = kv_ref[...]
    k = _rope(kv[:, :A_KV_WIDTH], cos, sin)
    v = kv[:, A_KV_WIDTH:]
    kb = k.astype(BF16)
    vb = v.astype(BF16)
    kswb = pltpu.roll(k, A_HALF, 1).astype(BF16)
    vswb = pltpu.roll(v, A_HALF, 1).astype(BF16)

    rows4 = nblk * A_BLOCK
    first = jnp.where(step > 0, 0, 1)
    slab = lax.broadcasted_iota(jnp.int32, (rows4, 1), 0) // A_BLOCK
    lo = lax.broadcasted_iota(jnp.int32, (A_BLOCK, LANES), 1) < A_HALF
    ones = jnp.ones((2 * A_BLOCK, LANES), BF16)

    def softmax_pv(lhs, k_all, v_aug, heads, mask):
        s = lax.dot_general(lhs, k_all, (((1,), (1,)), ((), ())), preferred_element_type=F32)
        s = s + mask
        sink = jnp.zeros((rows4, 1), F32)
        for n, h in enumerate(heads):
            sink = jnp.where(slab == n, sinks_ref[h], sink)
        m = jnp.maximum(jnp.max(s, axis=-1, keepdims=True), sink)
        p = jnp.exp(s - m).astype(BF16)
        o = jnp.dot(p, v_aug, preferred_element_type=F32)
        den = o[:, LANES:] + jnp.exp(sink - m)
        return o[:, :LANES] / den

    kp, kpsw, vp, vpsw = prev[0], prev[1], prev[2], prev[3]
    for j in range(A_TQ // A_BLOCK):
        rows = slice(j * A_BLOCK, (j + 1) * A_BLOCK)
        kc, kcsw, vc, vcsw = kb[rows], kswb[rows], vb[rows], vswb[rows]
        k_all = jnp.concatenate([kp, kc], axis=0)
        k_all_sw = jnp.concatenate([kpsw, kcsw], axis=0)
        v_aug = jnp.concatenate([jnp.concatenate([vp, vc], axis=0), ones], axis=1)
        v_aug_sw = jnp.concatenate([jnp.concatenate([vpsw, vcsw], axis=0), ones], axis=1)
        mask = bias_ref[0] if j > 0 else bias_ref[first]

        qj = q[rows]
        blocks = [qj[:, n * LANES:(n + 1) * LANES] for n in range(nblk)]
        zero = jnp.zeros_like(blocks[0])
        q_lo = [jnp.where(lo, x, zero) for x in blocks]
        q_hi = [jnp.where(lo, zero, x) for x in blocks]
        half = nblk // A_KV_HEADS
        lhs_a = jnp.concatenate(q_lo[:half] + q_hi[half:], axis=0)
        lhs_b = jnp.concatenate(q_hi[:half] + q_lo[half:], axis=0)
        heads_a = [2 * n for n in range(half)] + [2 * n + 1 for n in range(half, nblk)]
        heads_b = [2 * n + 1 for n in range(half)] + [2 * n for n in range(half, nblk)]
        oa = softmax_pv(lhs_a, k_all, v_aug, heads_a, mask)
        ob = softmax_pv(lhs_b, k_all_sw, v_aug_sw, heads_b, mask)
        outs = []
        for n in range(nblk):
            a_n = oa[n * A_BLOCK:(n + 1) * A_BLOCK]
            b_n = ob[n * A_BLOCK:(n + 1) * A_BLOCK]
            outs.append(jnp.where(lo, a_n, b_n) if n < half else jnp.where(lo, b_n, a_n))
        out = jnp.concatenate(outs, axis=1)
        o_ref[rows, :] = (out * _silu(az_ref[rows, :])).astype(BF16)
        kp, kpsw, vp, vpsw = kc, kcsw, vc, vcsw

    prev[0] = kp
    prev[1] = kpsw
    prev[2] = vp
    prev[3] = vpsw


def _attention(proj, sinks, cos_t, sin_t, batch, seq):
    nq = seq // A_TQ
    T = batch * seq
    rows4 = (A_WIDTH // LANES) * A_BLOCK
    qi = np.arange(rows4)[:, None] % A_BLOCK
    kj = np.arange(2 * A_BLOCK)[None, :]
    band = (qi - kj + A_BLOCK >= 0) & (qi - kj + A_BLOCK < WINDOW)
    bias = np.where(np.stack([band, band & (kj >= A_BLOCK)]), 0.0, -np.inf).astype(np.float32)
    assert (A_HEAD_DIM * 2 == LANES and A_KV_WIDTH == LANES
            and (A_HEADS // A_KV_HEADS) % 2 == 0 and WINDOW == A_BLOCK)
    return pl.pallas_call(
        _attn_kernel,
        grid=(batch, nq),
        in_specs=[
            pl.BlockSpec(memory_space=pltpu.SMEM),
            pl.BlockSpec((A_TQ, A_WIDTH), lambda b, i: (b * nq + i, COL_AQ // A_WIDTH)),
            pl.BlockSpec((A_TQ, 2 * A_KV_WIDTH), lambda b, i: (b * nq + i, COL_KV // (2 * A_KV_WIDTH))),
            pl.BlockSpec((A_TQ, A_WIDTH), lambda b, i: (b * nq + i, COL_AZ // A_WIDTH)),
            pl.BlockSpec((A_TQ, LANES), lambda b, i: (i, 0)),
            pl.BlockSpec((A_TQ, LANES), lambda b, i: (i, 0)),
            pl.BlockSpec((2, rows4, 2 * A_BLOCK), lambda b, i: (0, 0, 0)),
        ],
        out_specs=pl.BlockSpec((A_TQ, A_WIDTH), lambda b, i: (b * nq + i, 0)),
        out_shape=jax.ShapeDtypeStruct((T, A_WIDTH), BF16),
        scratch_shapes=[pltpu.VMEM((4, A_BLOCK, LANES), BF16)],
        compiler_params=pltpu.CompilerParams(
            dimension_semantics=("parallel", "arbitrary"),
            vmem_limit_bytes=VMEM_LIMIT),
        name="swa_attention",
    )(sinks, proj, proj, proj, cos_t, sin_t, jnp.asarray(bias))


R_TB = 256
SQRT_FLOOR = 1e-30


def _causal_conv(buf_ref, w, tb):
    xext = buf_ref[0:SUBLANES + tb, :]
    acc = xext[SUBLANES:, :] * w[CONV_WIDTH - 1:CONV_WIDTH, :]
    for k in range(CONV_WIDTH - 1):
        shifted = pltpu.roll(xext, CONV_WIDTH - 1 - k, 0)[SUBLANES:, :]
        acc = acc + shifted * w[k:k + 1, :]
    return acc


def _rglru_kernel(rx_ref, rz_ref, cw_ref, cb_ref, wa_ref, ba_ref, wx_ref, bx_ref, lam_ref,
                  o_ref, xbuf, acum, ucum, hcarry):
    tb = R_TB

    @pl.when(pl.program_id(1) == 0)
    def _():
        xbuf[0:SUBLANES, :] = jnp.zeros((SUBLANES, R_WIDTH), F32)
        hcarry[...] = jnp.zeros_like(hcarry)

    xbuf[SUBLANES:SUBLANES + tb, :] = rx_ref[...]
    xr = _causal_conv(xbuf, cw_ref[...], tb) + cb_ref[...]
    xbuf[0:SUBLANES, :] = xbuf[tb:tb + SUBLANES, :]

    xrb = xr.astype(BF16)
    ra, ia = [], []
    for n in range(R_BLOCKS):
        xb = xrb[:, n * R_BLOCK_DIM:(n + 1) * R_BLOCK_DIM]
        ra.append(jnp.dot(xb, wa_ref[n], preferred_element_type=F32))
        ia.append(jnp.dot(xb, wx_ref[n], preferred_element_type=F32))
    r = _sigmoid(jnp.concatenate(ra, axis=1) + ba_ref[...])
    ig = _sigmoid(jnp.concatenate(ia, axis=1) + bx_ref[...])
    log_a = (-R_C) * r * _softplus(-lam_ref[...])
    a = jnp.exp(log_a)
    s = jnp.maximum(1.0 - a * a, 0.0)
    u = (s * lax.rsqrt(jnp.maximum(s, SQRT_FLOOR))) * (ig * xr)

    row8 = lax.broadcasted_iota(jnp.int32, (tb, R_WIDTH), 0) % SUBLANES
    for k in (1, 2, 4):
        keep = row8 >= k
        a_sh = jnp.where(keep, pltpu.roll(a, k, 0), 1.0)
        u_sh = jnp.where(keep, pltpu.roll(u, k, 0), 0.0)
        u = a * u_sh + u
        a = a * a_sh
    acum[...] = a
    ucum[...] = u

    def body(g, hprev):
        r0 = pl.multiple_of(g * SUBLANES, SUBLANES)
        h = acum[pl.ds(r0, SUBLANES), :] * hprev + ucum[pl.ds(r0, SUBLANES), :]
        ucum[pl.ds(r0, SUBLANES), :] = h
        return jnp.broadcast_to(h[SUBLANES - 1:SUBLANES, :], (SUBLANES, R_WIDTH))

    hlast = lax.fori_loop(0, tb // SUBLANES, body, hcarry[...])
    hcarry[...] = hlast
    o_ref[...] = (ucum[...] * _silu(rz_ref[...])).astype(BF16)


def _rglru(proj, cw, cb, wa, ba, wx, bx, lam, batch, seq):
    tb = R_TB
    nblk = seq // tb
    T = batch * seq
    vec = lambda: pl.BlockSpec((1, R_WIDTH), lambda b, i: (0, 0))
    return pl.pallas_call(
        _rglru_kernel,
        grid=(batch, nblk),
        in_specs=[
            pl.BlockSpec((tb, R_WIDTH), lambda b, i: (b * nblk + i, COL_RX // R_WIDTH)),
            pl.BlockSpec((tb, R_WIDTH), lambda b, i: (b * nblk + i, COL_RZ // R_WIDTH)),
            pl.BlockSpec((CONV_WIDTH, R_WIDTH), lambda b, i: (0, 0)),
            vec(),
            pl.BlockSpec((R_BLOCKS, R_BLOCK_DIM, R_BLOCK_DIM), lambda b, i: (0, 0, 0)),
            vec(),
            pl.BlockSpec((R_BLOCKS, R_BLOCK_DIM, R_BLOCK_DIM), lambda b, i: (0, 0, 0)),
            vec(),
            vec(),
        ],
        out_specs=pl.BlockSpec((tb, R_WIDTH), lambda b, i: (b * nblk + i, 0)),
        out_shape=jax.ShapeDtypeStruct((T, R_WIDTH), BF16),
        scratch_shapes=[
            pltpu.VMEM((SUBLANES + tb, R_WIDTH), F32),
            pltpu.VMEM((tb, R_WIDTH), F32),
            pltpu.VMEM((tb, R_WIDTH), F32),
            pltpu.VMEM((SUBLANES, R_WIDTH), F32),
        ],
        compiler_params=pltpu.CompilerParams(
            dimension_semantics=("parallel", "arbitrary"),
            vmem_limit_bytes=VMEM_LIMIT),
        name="rglru",
    )(proj, proj, cw, cb, wa, ba, wx, bx, lam)


G_TB = 512


def _l2norm(x):
    return x * lax.rsqrt(jnp.sum(x * x, axis=-1, keepdims=True) + RMS_EPS)


def _bdot(a, b):
    return jnp.dot(a.astype(BF16), b.astype(BF16), preferred_element_type=F32)


def _unit_lower_solves(ms, rhss):
    c = ms[0].shape[0]
    eye = (lax.broadcasted_iota(jnp.int32, (c, c), 0) ==
           lax.broadcasted_iota(jnp.int32, (c, c), 1)).astype(F32)
    p_s = [eye - m for m in ms]
    a_s = [_bdot(m, m) for m in ms]
    k = 2
    while k < c // 2:
        st = [_bdot(jnp.concatenate([a, p], axis=0), a) for a, p in zip(a_s, p_s)]
        p_s = [p + s[c:] for p, s in zip(p_s, st)]
        a_s = [s[:c] for s in st]
        k *= 2
    ys = [r + _bdot(a, r) for a, r in zip(a_s, rhss)]
    return [_bdot(p, y) for p, y in zip(p_s, ys)]


G_GROUP = 2


def _gdn_kernel(gq_ref, gk_ref, gv_ref, gz_ref, sm_ref, cwq_ref, cwk_ref, cwv_ref,
                alog_ref, dtb_ref, nw_ref, o_ref,
                qbuf, kbuf, vbuf, qs, ks, vs, bg, us, ws, qds, qks, kdts, gls, state):
    tb = G_TB
    C = G_CHUNK
    nb = gq_ref.shape[0]

    @pl.when(pl.program_id(0) == 0)
    def _():
        zeros = jnp.zeros((nb, SUBLANES, G_WIDTH), F32)
        qbuf[:, 0:SUBLANES, :] = zeros
        kbuf[:, 0:SUBLANES, :] = zeros
        vbuf[:, 0:SUBLANES, :] = zeros
        state[...] = jnp.zeros_like(state)

    lane = lax.broadcasted_iota(jnp.int32, (tb, LANES), 1)
    for b in range(nb):
        for src, buf, cw, dst in ((gq_ref, qbuf, cwq_ref, qs), (gk_ref, kbuf, cwk_ref, ks),
                                  (gv_ref, vbuf, cwv_ref, vs)):
            buf[b, SUBLANES:SUBLANES + tb, :] = src[b]
            dst[b] = _silu(_causal_conv(buf.at[b], cw[...], tb))
            buf[b, 0:SUBLANES, :] = buf[b, tb:tb + SUBLANES, :]
        small = sm_ref[b]
        g_all = -jnp.exp(alog_ref[...]) * _softplus(small + dtb_ref[...])
        bg[b] = jnp.where(lane < G_HEADS, _sigmoid(small), g_all)

    ri = lax.broadcasted_iota(jnp.int32, (C, C), 0)
    ci = lax.broadcasted_iota(jnp.int32, (C, C), 1)
    tril = ri >= ci
    strict = ri > ci
    tril_f = tril.astype(F32)

    def intra(grp, carry):
        dests, ms, rhss = [], [], []
        for b in range(nb):
            for cc in range(G_GROUP):
                c = grp * G_GROUP + cc
                r0 = pl.multiple_of(c * C, C)
                bgc = bg[b, pl.ds(r0, C), :]
                gcum_col = jnp.dot(tril_f, bgc, precision=HIGHEST, preferred_element_type=F32)
                gcum_row = gcum_col.T[:SUBLANES, :]
                for h in range(G_HEADS):
                    cols = slice(h * G_HEAD_DIM, (h + 1) * G_HEAD_DIM)
                    qn = _l2norm(qs[b, pl.ds(r0, C), cols]) * (G_HEAD_DIM ** -0.5)
                    kn = _l2norm(ks[b, pl.ds(r0, C), cols])
                    v = vs[b, pl.ds(r0, C), cols]
                    beta = jnp.broadcast_to(bgc[:, h:h + 1], (C, G_HEAD_DIM))
                    gc = jnp.broadcast_to(gcum_col[:, G_HEADS + h:G_HEADS + h + 1], (C, G_HEAD_DIM))
                    gr = gcum_row[G_HEADS + h:G_HEADS + h + 1, :]
                    g_last = gc[C - 1:C, :]
                    decay = jnp.where(tril, jnp.exp(gc[:, :C] - gr), 0.0)
                    eg = jnp.exp(gc)
                    knb = kn.astype(BF16)
                    qk_kk = lax.dot_general(jnp.concatenate([qn.astype(BF16), knb], axis=0), knb,
                                            (((1,), (1,)), ((), ())), preferred_element_type=F32)
                    qks[b, pl.ds(r0, C), h * C:(h + 1) * C] = (qk_kk[:C] * decay).astype(BF16)
                    ms.append(jnp.where(strict, qk_kk[C:] * beta[:, :C] * decay, 0.0))
                    qds[b, pl.ds(r0, C), cols] = (qn * eg).astype(BF16)
                    kdts[b, c * G_HEADS + h] = (kn * jnp.exp(g_last - gc)).T.astype(BF16)
                    gls[b, c * G_HEADS + h] = jnp.broadcast_to(jnp.exp(g_last), (SUBLANES, LANES))
                    rhss.append(jnp.concatenate([v * beta, kn * (beta * eg)], axis=1))
                    dests.append((b, r0, cols))
        for (b, r0, cols), uw in zip(dests, _unit_lower_solves(ms, rhss)):
            us[b, pl.ds(r0, C), cols] = uw[:, :G_HEAD_DIM]
            ws[b, pl.ds(r0, C), cols] = uw[:, G_HEAD_DIM:].astype(BF16)
        return carry

    for grp in range(tb // (C * G_GROUP)):
        intra(grp, 0)

    chains = [(b, h) for b in range(nb) for h in range(G_HEADS)]

    def inter(c, carry):
        r0 = pl.multiple_of(c * C, C)
        rows = pl.ds(r0, C)
        hcols = lambda h: slice(h * G_HEAD_DIM, (h + 1) * G_HEAD_DIM)
        s_old = [state[b, h] for b, h in chains]
        sb = [s.astype(BF16) for s in s_old]
        w_s = [jnp.dot(ws[b, rows, hcols(h)], s, preferred_element_type=F32)
               for (b, h), s in zip(chains, sb)]
        vnb = [(us[b, rows, hcols(h)] - x).astype(BF16) for (b, h), x in zip(chains, w_s)]
        upd = [jnp.dot(kdts[b, c * G_HEADS + h], v, preferred_element_type=F32)
               for (b, h), v in zip(chains, vnb)]
        for (b, h), s, x in zip(chains, s_old, upd):
            state[b, h] = s * gls[b, c * G_HEADS + h][0:1, :] + x
        for (b, h), s, v in zip(chains, sb, vnb):
            o = (jnp.dot(qds[b, rows, hcols(h)], s, preferred_element_type=F32)
                 + jnp.dot(qks[b, rows, h * C:(h + 1) * C], v, preferred_element_type=F32))
            on = o * lax.rsqrt(jnp.mean(o * o, axis=-1, keepdims=True) + RMS_EPS) * nw_ref[...]
            o_ref[b, rows, hcols(h)] = (on * _silu(gz_ref[b, rows, hcols(h)])).astype(BF16)
        return carry

    lax.fori_loop(0, tb // C, inter, 0, unroll=True)


def _gdn(proj, cwq, cwk, cwv, alog, dtb, nw, batch, seq):
    tb = G_TB
    nchunk = tb // G_CHUNK
    proj3 = proj.reshape(batch, seq, N_PROJ)
    seg = lambda col: pl.BlockSpec((batch, tb, G_WIDTH), lambda i: (0, i, col // G_WIDTH))
    cw = lambda: pl.BlockSpec((CONV_WIDTH, G_WIDTH), lambda i: (0, 0))
    vec = lambda: pl.BlockSpec((1, LANES), lambda i: (0, 0))
    out = pl.pallas_call(
        _gdn_kernel,
        grid=(seq // tb,),
        in_specs=[seg(COL_GQ), seg(COL_GK), seg(COL_GV), seg(COL_GZ),
                  pl.BlockSpec((batch, tb, LANES), lambda i: (0, i, COL_SMALL // LANES)),
                  cw(), cw(), cw(), vec(), vec(), vec()],
        out_specs=pl.BlockSpec((batch, tb, G_WIDTH), lambda i: (0, i, 0)),
        out_shape=jax.ShapeDtypeStruct((batch, seq, G_WIDTH), BF16),
        scratch_shapes=[
            pltpu.VMEM((batch, SUBLANES + tb, G_WIDTH), F32),
            pltpu.VMEM((batch, SUBLANES + tb, G_WIDTH), F32),
            pltpu.VMEM((batch, SUBLANES + tb, G_WIDTH), F32),
            pltpu.VMEM((batch, tb, G_WIDTH), F32),
            pltpu.VMEM((batch, tb, G_WIDTH), F32),
            pltpu.VMEM((batch, tb, G_WIDTH), F32),
            pltpu.VMEM((batch, tb, LANES), F32),
            pltpu.VMEM((batch, tb, G_WIDTH), F32),
            pltpu.VMEM((batch, tb, G_WIDTH), BF16),
            pltpu.VMEM((batch, tb, G_WIDTH), BF16),
            pltpu.VMEM((batch, tb, G_HEADS * G_CHUNK), BF16),
            pltpu.VMEM((batch, nchunk * G_HEADS, G_HEAD_DIM, G_CHUNK), BF16),
            pltpu.VMEM((batch, nchunk * G_HEADS, SUBLANES, LANES), F32),
            pltpu.VMEM((batch, G_HEADS, G_HEAD_DIM, G_HEAD_DIM), F32),
        ],
        compiler_params=pltpu.CompilerParams(
            dimension_semantics=("arbitrary",),
            vmem_limit_bytes=VMEM_LIMIT),
        name="gated_deltanet",
    )(proj3, proj3, proj3, proj3, proj3, cwq, cwk, cwv, alog, dtb, nw)
    return out.reshape(batch * seq, G_WIDTH)


OUT_TM = 512


def _outproj_kernel(ya_ref, yr_ref, yg_ref, x_ref, w_ref, g_ref, b_ref, o_ref, ob_ref, wb):
    @pl.when(pl.program_id(0) == 0)
    def _():
        wb[...] = w_ref[...].astype(BF16)

    half = OUT_TM // 2
    for r in range(2):
        rows = slice(r * half, (r + 1) * half)
        y = jnp.dot(ya_ref[rows, :], wb[0:A_WIDTH, :], preferred_element_type=F32)
        y += jnp.dot(yr_ref[rows, :], wb[A_WIDTH:A_WIDTH + R_WIDTH, :], preferred_element_type=F32)
        y += jnp.dot(yg_ref[rows, :], wb[A_WIDTH + R_WIDTH:MIX_WIDTH, :], preferred_element_type=F32)
        z = DEEPNORM_ALPHA * x_ref[rows, :] + y
        mu = jnp.mean(z, axis=-1, keepdims=True)
        zc = z - mu
        var = jnp.mean(zc * zc, axis=-1, keepdims=True)
        out = zc * lax.rsqrt(var + LN_EPS) * g_ref[...] + b_ref[...]
        o_ref[rows, :] = out
        ob_ref[rows, :] = out.astype(BF16)


def _outproj(ya, yr, yg, x, w, layer, g, b):
    T = x.shape[0]
    tm = OUT_TM
    row = lambda width: pl.BlockSpec((tm, width), lambda i: (i, 0))
    vec = lambda: pl.BlockSpec((1, D_MODEL), lambda i: (0, 0))
    return pl.pallas_call(
        _outproj_kernel,
        grid=(T // tm,),
        in_specs=[row(A_WIDTH), row(R_WIDTH), row(G_WIDTH), row(D_MODEL),
                  pl.BlockSpec((None, MIX_WIDTH, D_MODEL), lambda i: (layer, 0, 0),
                               pipeline_mode=pl.Buffered(1)),
                  vec(), vec()],
        out_specs=[row(D_MODEL), row(D_MODEL)],
        out_shape=[jax.ShapeDtypeStruct((T, D_MODEL), F32),
                   jax.ShapeDtypeStruct((T, D_MODEL), BF16)],
        scratch_shapes=[pltpu.VMEM((MIX_WIDTH, D_MODEL), BF16)],
        compiler_params=pltpu.CompilerParams(
            dimension_semantics=("arbitrary",),
            vmem_limit_bytes=VMEM_LIMIT),
        name="outproj_deepnorm",
    )(ya, yr, yg, x, w, g, b)


def _rope_tables(seq):
    half = A_HEAD_DIM // 2
    inv = 1.0 / (ROPE_THETA ** (jnp.arange(0, A_HEAD_DIM, 2, dtype=F32) / A_HEAD_DIM))
    ang = jnp.arange(seq, dtype=F32)[:, None] * inv[None, :]
    cos, sin = jnp.cos(ang), jnp.sin(ang)
    reps = LANES // A_HEAD_DIM
    cos_t = jnp.tile(jnp.concatenate([cos, cos], axis=1), (1, reps))
    sin_t = jnp.tile(jnp.concatenate([-sin, sin], axis=1), (1, reps))
    assert cos_t.shape == (seq, LANES) and half * 2 == A_HEAD_DIM
    return cos_t, sin_t


def _lane_vec(v, offset):
    return jnp.zeros((1, LANES), F32).at[0, offset:offset + v.shape[0]].set(v.astype(F32))


@jax.jit
def _forward(x, w_in, sinks, r_conv_w, r_conv_b, r_wa, r_ba, r_wx, r_bx, r_lam,
             g_conv_w, g_a_log, g_dt_bias, g_norm_w, w_out, ln_g, ln_b):
    batch, seq, _ = x.shape
    T = batch * seq
    cos_t, sin_t = _rope_tables(seq)
    w_in_t = jnp.swapaxes(w_in, 1, 2)
    xf = x.reshape(T, D_MODEL)
    xb = xf.astype(BF16)
    for l in range(DEPTH):
        proj = _inproj(xb, w_in_t, l)
        ya = _attention(proj, sinks[l], cos_t, sin_t, batch, seq)
        yr = _rglru(proj, r_conv_w[l], r_conv_b[l][None, :], r_wa[l].astype(BF16), r_ba[l][None, :],
                    r_wx[l].astype(BF16), r_bx[l][None, :], r_lam[l][None, :], batch, seq)
        gcw = g_conv_w[l]
        yg = _gdn(proj, gcw[:, :G_WIDTH], gcw[:, G_WIDTH:2 * G_WIDTH], gcw[:, 2 * G_WIDTH:],
                  _lane_vec(g_a_log[l], G_HEADS), _lane_vec(g_dt_bias[l], G_HEADS),
                  g_norm_w[l][None, :], batch, seq)
        xf, xb = _outproj(ya, yr, yg, xf, w_out, l, ln_g[l][None, :], ln_b[l][None, :])
    return xf.reshape(batch, seq, D_MODEL)


def kernel(x, w_in, sinks, r_conv_w, r_conv_b, r_wa, r_ba, r_wx, r_bx, r_lam, g_conv_w, g_a_log,
           g_dt_bias, g_norm_w, w_out, ln_g, ln_b):
    return _forward(x, w_in, sinks, r_conv_w, r_conv_b, r_wa, r_ba, r_wx, r_bx, r_lam,
                    g_conv_w, g_a_log, g_dt_bias, g_norm_w, w_out, ln_g, ln_b)
```
